```python
import math
import jax, jax.numpy as jnp
from jax import lax
import numpy as np

D_MODEL = 1024
BATCH = 8
SEQ = 8192
DEPTH = 1
DEC_BATCH = 4
DEC_SEQ = 8192
PAST_LEN = 128

HEAD_DIM = 64
A_HEADS = 8
A_KV = 2
B_HEADS = 8
B_KV = 2
A_WIDTH = A_HEADS * HEAD_DIM
B_WIDTH = B_HEADS * HEAD_DIM
WINDOW = 128
BLOCK = 128
N_META = 16
GRID_W = 64
ROPE_THETA = 10000.0
LN_EPS = 1e-5
RMS_EPS = 1e-6
NEG_INF = -1e30
ALPHA = (2.0 * DEPTH) ** 0.25
BETA = (8.0 * DEPTH) ** -0.25
IN_SPLITS = (A_HEADS * HEAD_DIM, A_KV * HEAD_DIM, A_KV * HEAD_DIM, A_WIDTH,
             B_HEADS * HEAD_DIM, B_KV * HEAD_DIM, B_KV * HEAD_DIM, B_WIDTH,
             D_MODEL, D_MODEL)
D_IN = sum(IN_SPLITS)

kernel_name = "hybrid_window_axial_gqa_encoder"


def _rope_angles(pos, dim):
    inv = ROPE_THETA ** (-jnp.arange(0, dim, 2, dtype=jnp.float32) / dim)
    return pos.astype(jnp.float32)[:, None] * inv[None, :]


def _rotate(x, ang):
    d2 = x.shape[-1] // 2
    cos = jnp.cos(ang)[:, None, :].astype(x.dtype)
    sin = jnp.sin(ang)[:, None, :].astype(x.dtype)
    x1, x2 = x[..., :d2], x[..., d2:]
    return jnp.concatenate([x1 * cos - x2 * sin, x1 * sin + x2 * cos], axis=-1)


def _axial_rope(x, row, col):
    h = x.shape[-1] // 2
    return jnp.concatenate([_rotate(x[..., :h], _rope_angles(row, h)),
                            _rotate(x[..., h:], _rope_angles(col, h))], axis=-1)


def _rms_norm(x, g):
    xf = x.astype(jnp.float32)
    y = xf * lax.rsqrt(jnp.mean(xf * xf, axis=-1, keepdims=True) + RMS_EPS) * g.astype(jnp.float32)
    return y.astype(x.dtype)


def _layer_norm(x, g, b):
    xf = x.astype(jnp.float32)
    mu = jnp.mean(xf, axis=-1, keepdims=True)
    var = jnp.mean(jnp.square(xf - mu), axis=-1, keepdims=True)
    y = (xf - mu) * lax.rsqrt(var + LN_EPS) * g.astype(jnp.float32) + b.astype(jnp.float32)
    return y.astype(x.dtype)


def _window_attention(q, k, v, sink):
    Bn, L, H, hd = q.shape
    KV = k.shape[2]
    G = H // KV
    S = L - N_META
    nb = S // BLOCK
    scale = hd ** -0.5
    qm, qr = q[:, :N_META], q[:, N_META:]
    km, kr = k[:, :N_META], k[:, N_META:]
    vm, vr = v[:, :N_META], v[:, N_META:]
    sink_g = sink.astype(jnp.float32).reshape(KV, G)

    qb = qr.reshape(Bn, nb, BLOCK, KV, G, hd)
    pad = ((0, 0), (BLOCK, BLOCK), (0, 0), (0, 0))
    kp = jnp.pad(kr, pad).reshape(Bn, nb + 2, BLOCK, KV, hd)
    vp = jnp.pad(vr, pad).reshape(Bn, nb + 2, BLOCK, KV, hd)
    kb = jnp.concatenate([kp[:, :-2], kp[:, 1:-1], kp[:, 2:]], axis=2)
    vb = jnp.concatenate([vp[:, :-2], vp[:, 1:-1], vp[:, 2:]], axis=2)
    s_band = jnp.einsum('bnqkgd,bnskd->bnkgqs', qb, kb).astype(jnp.float32) * scale
    blk = jnp.arange(nb)
    key_idx = blk[:, None] * BLOCK - BLOCK + jnp.arange(3 * BLOCK)[None, :]
    q_idx = blk[:, None] * BLOCK + jnp.arange(BLOCK)[None, :]
    rel = key_idx[:, None, :] - q_idx[:, :, None]
    valid = (jnp.abs(rel) <= WINDOW) & (key_idx[:, None, :] >= 0) & (key_idx[:, None, :] < S)
    s_band = jnp.where(valid[None, :, None, None], s_band, NEG_INF)
    s_meta = jnp.einsum('bnqkgd,bmkd->bnkgqm', qb, km).astype(jnp.float32) * scale
    s_sink = jnp.broadcast_to(sink_g[None, None, :, :, None, None], s_meta.shape[:-1] + (1,))
    p = jax.nn.softmax(jnp.concatenate([s_meta, s_band, s_sink], axis=-1), axis=-1).astype(v.dtype)
    o_real = (jnp.einsum('bnkgqm,bmkd->bnqkgd', p[..., :N_META], vm)
              + jnp.einsum('bnkgqs,bnskd->bnqkgd', p[..., N_META:N_META + 3 * BLOCK], vb))
    o_real = o_real.reshape(Bn, S, H, hd)

    kmq = jnp.concatenate([km, kr[:, :BLOCK]], axis=1)
    vmq = jnp.concatenate([vm, vr[:, :BLOCK]], axis=1)
    qmg = qm.reshape(Bn, N_META, KV, G, hd)
    s_m = jnp.einsum('bqkgd,bskd->bkgqs', qmg, kmq).astype(jnp.float32) * scale
    kpos = jnp.arange(N_META + BLOCK)
    qpos = jnp.arange(N_META)
    valid_m = (jnp.abs(kpos[None, :] - qpos[:, None]) <= WINDOW) | (kpos[None, :] < N_META)
    s_m = jnp.where(valid_m[None, None, None], s_m, NEG_INF)
    s_ms = jnp.broadcast_to(sink_g[None, :, :, None, None], s_m.shape[:-1] + (1,))
    p_m = jax.nn.softmax(jnp.concatenate([s_m, s_ms], axis=-1), axis=-1)[..., :-1].astype(v.dtype)
    o_meta = jnp.einsum('bkgqs,bskd->bqkgd', p_m, vmq).reshape(Bn, N_META, H, hd)
    return jnp.concatenate([o_meta, o_real], axis=1)


def _dense_block_attention(q, k, v):
    Bn, L, H, hd = q.shape
    KV = k.shape[2]
    G = H // KV
    S = L - N_META
    nb = S // BLOCK
    scale = hd ** -0.5

    def attend(qblk):
        s = jnp.einsum('bqkgd,bskd->bkgqs', qblk, k).astype(jnp.float32) * scale
        p = jax.nn.softmax(s, axis=-1).astype(v.dtype)
        return jnp.einsum('bkgqs,bskd->bqkgd', p, v)

    o_meta = attend(q[:, :N_META].reshape(Bn, N_META, KV, G, hd))
    qr = q[:, N_META:].reshape(Bn, nb, BLOCK, KV, G, hd).transpose(1, 0, 2, 3, 4, 5)
    o_real = lax.map(attend, qr)
    o_real = o_real.transpose(1, 0, 2, 3, 4, 5).reshape(Bn, S, H, hd)
    return jnp.concatenate([o_meta.reshape(Bn, N_META, H, hd), o_real], axis=1)


def _layer(h, w_in, sink, q_gain, k_gain, w_ba, w_bb, w_o, ln_g, ln_b, ang_a, row, col):
    Bn, L, _ = h.shape
    proj = h @ w_in
    cuts = [int(c) for c in np.cumsum(IN_SPLITS)[:-1]]
    qa, ka, va, za, qb, kb, vb, zb, ga, gb = jnp.split(proj, cuts, axis=-1)

    qa = _rotate(qa.reshape(Bn, L, A_HEADS, HEAD_DIM), ang_a)
    ka = _rotate(ka.reshape(Bn, L, A_KV, HEAD_DIM), ang_a)
    va = va.reshape(Bn, L, A_KV, HEAD_DIM)
    ya = _window_attention(qa, ka, va, sink).reshape(Bn, L, A_WIDTH) * jax.nn.silu(za)
    ua = ya @ w_ba

    qb = _axial_rope(_rms_norm(qb.reshape(Bn, L, B_HEADS, HEAD_DIM), q_gain), row, col)
    kb = _axial_rope(_rms_norm(kb.reshape(Bn, L, B_KV, HEAD_DIM), k_gain), row, col)
    vb = vb.reshape(Bn, L, B_KV, HEAD_DIM)
    yb = _dense_block_attention(qb, kb, vb).reshape(Bn, L, B_WIDTH) * jax.nn.silu(zb)
    ub = yb @ w_bb

    merged = jax.nn.sigmoid(ga) * ua + jax.nn.sigmoid(gb) * ub
    out = merged @ w_o
    return _layer_norm(ALPHA * h + out, ln_g, ln_b)


def _encode(x, meta_tokens, w_in, attn_a_sink, q_norm_b, k_norm_b, w_branch_a, w_branch_b,
            w_out, ln_gain, ln_bias):
    Bn, S, D = x.shape
    ROWS = S // GRID_W
    L = S + N_META
    meta = jnp.broadcast_to(meta_tokens.astype(x.dtype)[None], (Bn, N_META, D))
    h = jnp.concatenate([meta, x], axis=1)
    ang_a = _rope_angles(jnp.arange(L), HEAD_DIM)
    meta_pos = jnp.arange(N_META) - N_META
    row = jnp.concatenate([meta_pos, jnp.repeat(jnp.arange(ROWS), GRID_W)])
    col = jnp.concatenate([meta_pos, jnp.tile(jnp.arange(GRID_W), ROWS)])
    for l in range(DEPTH):
        h = _layer(h, w_in[l], attn_a_sink[l], q_norm_b[l], k_norm_b[l], w_branch_a[l],
                   w_branch_b[l], w_out[l], ln_gain[l], ln_bias[l], ang_a, row, col)
    return h[:, N_META:]


def setup_inputs(seed: int = 0) -> dict:
    key = jax.random.key(seed)
    ks = jax.random.split(key, 13)
    f32 = jnp.float32
    return {
        "x_prompt": jax.random.normal(ks[0], (BATCH, SEQ, D_MODEL), f32),
        "x_sample": jax.random.normal(ks[1], (DEC_BATCH, DEC_SEQ, D_MODEL), f32),
        "meta_tokens": jax.random.normal(ks[2], (N_META, D_MODEL), f32),
        "w_in": jax.random.normal(ks[3], (DEPTH, D_MODEL, D_IN), f32) * D_MODEL ** -0.5,
        "attn_a_sink": jax.random.normal(ks[4], (DEPTH, A_HEADS), f32) * 0.5,
        "q_norm_b": 1.0 + 0.02 * jax.random.normal(ks[5], (DEPTH, HEAD_DIM), f32),
        "k_norm_b": 1.0 + 0.02 * jax.random.normal(ks[6], (DEPTH, HEAD_DIM), f32),
        "w_branch_a": jax.random.normal(ks[7], (DEPTH, A_WIDTH, D_MODEL), f32) * (A_WIDTH ** -0.5 * BETA),
        "w_branch_b": jax.random.normal(ks[8], (DEPTH, B_WIDTH, D_MODEL), f32) * (B_WIDTH ** -0.5 * BETA),
        "w_out": jax.random.normal(ks[9], (DEPTH, D_MODEL, D_MODEL), f32) * (D_MODEL ** -0.5 * BETA),
        "ln_gain": 1.0 + 0.02 * jax.random.normal(ks[10], (DEPTH, D_MODEL), f32),
        "ln_bias": 0.02 * jax.random.normal(ks[11], (DEPTH, D_MODEL), f32),
    }


def reference(x_prompt, x_sample, meta_tokens, w_in, attn_a_sink, q_norm_b, k_norm_b,
              w_branch_a, w_branch_b, w_out, ln_gain, ln_bias):
    y_prompt = _encode(x_prompt, meta_tokens, w_in, attn_a_sink, q_norm_b, k_norm_b,
                       w_branch_a, w_branch_b, w_out, ln_gain, ln_bias)
    y_sample = _encode(x_sample, meta_tokens, w_in, attn_a_sink, q_norm_b, k_norm_b,
                       w_branch_a, w_branch_b, w_out, ln_gain, ln_bias)
    return (y_prompt, y_sample)
```

```python
import functools

import numpy as np
import jax
import jax.numpy as jnp
from jax import lax
from jax.experimental import pallas as pl
from jax.experimental.pallas import tpu as pltpu

D_MODEL = 1024
HEAD_DIM = 64
N_HEADS = 8
N_KV = 2
GROUP = N_HEADS // N_KV
WIDTH = N_HEADS * HEAD_DIM
KV_WIDTH = N_KV * HEAD_DIM
WINDOW = 128
BLOCK = 128
N_META = 16
GRID_W = 64
ROPE_THETA = 10000.0
LN_EPS = 1e-5
RMS_EPS = 1e-6
NEG_INF = -1e30
DEPTH = 1
ALPHA = (2.0 * DEPTH) ** 0.25
SCALE = HEAD_DIM ** -0.5

LANES = 128
VMEM_LIMIT = 56 * 1024 * 1024

OFF_QA = 0
OFF_KA = OFF_QA + WIDTH
OFF_VA = OFF_KA + KV_WIDTH
OFF_ZA = OFF_VA + KV_WIDTH
OFF_QB = OFF_ZA + WIDTH
OFF_KB = OFF_QB + WIDTH
OFF_VB = OFF_KB + KV_WIDTH
OFF_ZB = OFF_VB + KV_WIDTH
OFF_G = OFF_ZB + WIDTH
D_IN = OFF_G + 2 * D_MODEL

BF16 = jnp.bfloat16
F32 = jnp.float32


def _rope_angles(pos, dim):
    inv = ROPE_THETA ** (-jnp.arange(0, dim, 2, dtype=F32) / dim)
    return pos.astype(F32)[:, None] * inv[None, :]


def _rope_tables(pos_a, row, col):
    lane = np.arange(LANES)
    ang_a = _rope_angles(pos_a, HEAD_DIM)
    cos_a = jnp.tile(jnp.cos(ang_a), (1, 4))
    sin_a = jnp.tile(jnp.sin(ang_a), (1, 4))
    lo_a = jnp.asarray((lane % 64) < 32)[None, :]
    h = HEAD_DIM // 2
    ang_r = _rope_angles(row, h)
    ang_c = _rope_angles(col, h)
    cos_b = jnp.tile(jnp.concatenate([jnp.cos(ang_r)] * 2 + [jnp.cos(ang_c)] * 2, axis=1), (1, 2))
    sin_b = jnp.tile(jnp.concatenate([jnp.sin(ang_r)] * 2 + [jnp.sin(ang_c)] * 2, axis=1), (1, 2))
    lo_b = jnp.asarray((lane % 32) < 16)[None, :]
    zero = jnp.zeros_like(sin_a)
    return jnp.concatenate([
        cos_a, jnp.where(lo_a, -sin_a, zero), jnp.where(lo_a, zero, sin_a),
        cos_b, jnp.where(lo_b, -sin_b, zero), jnp.where(lo_b, zero, sin_b)], axis=1)


def _rope(x, cos, sin_lo, sin_hi, half):
    return (x * cos + pltpu.roll(x, LANES - half, 1) * sin_lo
            + pltpu.roll(x, half, 1) * sin_hi)


def _head_rms(x, gain):
    xx = x * x
    s0 = jnp.sum(xx[:, :HEAD_DIM], axis=1, keepdims=True)
    s1 = jnp.sum(xx[:, HEAD_DIM:], axis=1, keepdims=True)
    lane = lax.broadcasted_iota(jnp.int32, x.shape, 1)
    ms = jnp.where(lane < HEAD_DIM, s0, s1) * (1.0 / HEAD_DIM)
    return x * lax.rsqrt(ms + RMS_EPS) * gain


def _proj_kernel(x_ref, w_ref, tab_ref, gain_ref,
                 qa_ref, ka_ref, va_ref, za_ref, qb_ref, kb_ref, vb_ref, zb_ref, g_ref):
    xb = x_ref[0].astype(BF16)

    def mm(off, n):
        return jnp.dot(xb, w_ref[:, off:off + n], preferred_element_type=F32)

    def tab(i):
        return tab_ref[:, i * LANES:(i + 1) * LANES]

    cos_a, slo_a, shi_a, cos_b, slo_b, shi_b = (tab(i) for i in range(6))
    gq = gain_ref[0:1, :]
    gk = gain_ref[1:2, :]

    def split_kv(ref, x):
        ref[0, 0] = x[:, :HEAD_DIM].astype(BF16)
        ref[0, 1] = x[:, HEAD_DIM:].astype(BF16)

    qa = mm(OFF_QA, WIDTH)
    for c in range(WIDTH // LANES):
        xc = qa[:, c * LANES:(c + 1) * LANES]
        qa_ref[0, :, c * LANES:(c + 1) * LANES] = (
            _rope(xc, cos_a, slo_a, shi_a, HEAD_DIM // 2) * SCALE).astype(BF16)
    kva = mm(OFF_KA, 2 * KV_WIDTH)
    split_kv(ka_ref, _rope(kva[:, :KV_WIDTH], cos_a, slo_a, shi_a, HEAD_DIM // 2))
    split_kv(va_ref, kva[:, KV_WIDTH:])
    za_ref[0] = mm(OFF_ZA, WIDTH)

    qb = mm(OFF_QB, WIDTH)
    for c in range(WIDTH // LANES):
        xc = _head_rms(qb[:, c * LANES:(c + 1) * LANES], gq)
        qb_ref[0, :, c * LANES:(c + 1) * LANES] = (
            _rope(xc, cos_b, slo_b, shi_b, HEAD_DIM // 4) * SCALE).astype(BF16)
    kvb = mm(OFF_KB, 2 * KV_WIDTH)
    split_kv(kb_ref, _rope(_head_rms(kvb[:, :KV_WIDTH], gk), cos_b, slo_b, shi_b, HEAD_DIM // 4))
    split_kv(vb_ref, kvb[:, KV_WIDTH:])
    zb_ref[0] = mm(OFF_ZB, WIDTH)

    for c in range(2 * D_MODEL // 512):
        g_ref[0, :, c * 512:(c + 1) * 512] = mm(OFF_G + c * 512, 512)


def _project(x, w_in_bf, tables, gains, tm):
    bn, s, _ = x.shape
    grid = (s // tm, bn)
    tok = lambda n: pl.BlockSpec((1, tm, n), lambda i, b: (b, i, 0))
    kv = pl.BlockSpec((1, N_KV, tm, HEAD_DIM), lambda i, b: (b, 0, i, 0))
    const = lambda shape: pl.BlockSpec(shape, lambda i, b: (0,) * len(shape))
    out_shape = (
        jax.ShapeDtypeStruct((bn, s, WIDTH), BF16),
        jax.ShapeDtypeStruct((bn, N_KV, s, HEAD_DIM), BF16),
        jax.ShapeDtypeStruct((bn, N_KV, s, HEAD_DIM), BF16),
        jax.ShapeDtypeStruct((bn, s, WIDTH), F32),
        jax.ShapeDtypeStruct((bn, s, WIDTH), BF16),
        jax.ShapeDtypeStruct((bn, N_KV, s, HEAD_DIM), BF16),
        jax.ShapeDtypeStruct((bn, N_KV, s, HEAD_DIM), BF16),
        jax.ShapeDtypeStruct((bn, s, WIDTH), F32),
        jax.ShapeDtypeStruct((bn, s, 2 * D_MODEL), F32),
    )
    return pl.pallas_call(
        _proj_kernel,
        grid=grid,
        in_specs=[tok(D_MODEL), const((D_MODEL, D_IN)),
                  pl.BlockSpec((tm, 6 * LANES), lambda i, b: (i, 0)),
                  const((2, LANES))],
        out_specs=(tok(WIDTH), kv, kv, tok(WIDTH), tok(WIDTH), kv, kv, tok(WIDTH),
                   tok(2 * D_MODEL)),
        out_shape=out_shape,
        compiler_params=pltpu.CompilerParams(
            dimension_semantics=("arbitrary", "arbitrary"), vmem_limit_bytes=VMEM_LIMIT),
        name="proj",
    )(x, w_in_bf, tables, gains)


def _stack_heads(q):
    return jnp.concatenate([q[:, h * HEAD_DIM:(h + 1) * HEAD_DIM] for h in range(GROUP)], axis=0)


def _unstack_heads(o, tq):
    return jnp.concatenate([o[h * tq:(h + 1) * tq] for h in range(GROUP)], axis=1)


def _qk(q, k):
    return lax.dot_general(q, k, (((1,), (1,)), ((), ())), preferred_element_type=F32)


def _silu(z):
    return z * jax.nn.sigmoid(z)


def _attn_a_kernel(q_ref, kp_ref, kc_ref, kn_ref, vp_ref, vc_ref, vn_ref, km_ref, vm_ref,
                   sink_ref, z_ref, o_ref, *, nb):
    i = pl.program_id(2)
    q4 = _stack_heads(q_ref[0])
    kb = jnp.concatenate([kp_ref[0, 0], kc_ref[0, 0], kn_ref[0, 0]], axis=0)
    vb = jnp.concatenate([vp_ref[0, 0], vc_ref[0, 0], vn_ref[0, 0]], axis=0)
    s = _qk(q4, kb)
    r = lax.broadcasted_iota(jnp.int32, (BLOCK, 3 * BLOCK), 0)
    c = lax.broadcasted_iota(jnp.int32, (BLOCK, 3 * BLOCK), 1)
    rel = c - BLOCK - r
    blk = i - 1 + c // BLOCK
    valid = (jnp.abs(rel) <= WINDOW) & (blk >= 0) & (blk < nb)
    s = jnp.where(valid[None], s.reshape(GROUP, BLOCK, 3 * BLOCK), NEG_INF)
    s = s.reshape(GROUP * BLOCK, 3 * BLOCK)
    sm = _qk(q4, km_ref[0, 0])
    sink = sink_ref[0]
    m = jnp.maximum(jnp.maximum(jnp.max(s, axis=1, keepdims=True),
                                jnp.max(sm, axis=1, keepdims=True)), sink)
    p = jnp.exp(s - m)
    pm = jnp.exp(sm - m)
    l = (jnp.sum(p, axis=1, keepdims=True) + jnp.sum(pm, axis=1, keepdims=True)
         + jnp.exp(sink - m))
    o = (jnp.dot(p.astype(BF16), vb, preferred_element_type=F32)
         + jnp.dot(pm.astype(BF16), vm_ref[0, 0], preferred_element_type=F32))
    o = _unstack_heads(o / l, BLOCK)
    o_ref[0] = (o * _silu(z_ref[0])).astype(BF16)


def _attention_a(qa, ka, va, km, vm, sink_rows, za):
    bn, s, _ = qa.shape
    nb = s // BLOCK
    grid = (bn, N_KV, nb)
    qspec = pl.BlockSpec((1, BLOCK, GROUP * HEAD_DIM), lambda b, g, i: (b, i, g))
    kprev = pl.BlockSpec((1, 1, BLOCK, HEAD_DIM), lambda b, g, i: (b, g, jnp.maximum(i - 1, 0), 0))
    kcur = pl.BlockSpec((1, 1, BLOCK, HEAD_DIM), lambda b, g, i: (b, g, i, 0))
    knext = pl.BlockSpec((1, 1, BLOCK, HEAD_DIM),
                         lambda b, g, i: (b, g, jnp.minimum(i + 1, nb - 1), 0))
    meta = pl.BlockSpec((1, 1, N_META, HEAD_DIM), lambda b, g, i: (0, g, 0, 0))
    sink = pl.BlockSpec((1, GROUP * BLOCK, 1), lambda b, g, i: (g, 0, 0))
    return pl.pallas_call(
        functools.partial(_attn_a_kernel, nb=nb),
        grid=grid,
        in_specs=[qspec, kprev, kcur, knext, kprev, kcur, knext, meta, meta, sink, qspec],
        out_specs=qspec,
        out_shape=jax.ShapeDtypeStruct((bn, s, WIDTH), BF16),
        compiler_params=pltpu.CompilerParams(
            dimension_semantics=("arbitrary", "arbitrary", "arbitrary"),
            vmem_limit_bytes=VMEM_LIMIT),
        name="attn_a",
    )(qa, ka, ka, ka, va, va, va, km, vm, sink_rows, za)


def _attn_b_kernel(q_ref, k_ref, v_ref, km_ref, vm_ref, z_ref, o_ref, *, tq, tk):
    q4 = _stack_heads(q_ref[0])
    n_chunks = k_ref.shape[2] // tk

    s = _qk(q4, km_ref[0, 0])
    m0 = jnp.max(s, axis=1, keepdims=True)
    p = jnp.exp(s - m0)
    l0 = jnp.sum(p, axis=1, keepdims=True)
    acc0 = jnp.dot(p.astype(BF16), vm_ref[0, 0], preferred_element_type=F32)

    def body(c, carry):
        m, l, acc = carry
        start = pl.multiple_of(c * tk, tk)
        k = k_ref[0, 0, pl.ds(start, tk), :]
        v = v_ref[0, 0, pl.ds(start, tk), :]
        s = _qk(q4, k)
        m_new = jnp.maximum(m, jnp.max(s, axis=1, keepdims=True))
        a = jnp.exp(m - m_new)
        p = jnp.exp(s - m_new)
        l = a * l + jnp.sum(p, axis=1, keepdims=True)
        acc = a * acc + jnp.dot(p.astype(BF16), v, preferred_element_type=F32)
        return m_new, l, acc

    _, l, acc = lax.fori_loop(0, n_chunks, body, (m0, l0, acc0))
    o = _unstack_heads(acc / l, tq)
    o_ref[0] = (o * _silu(z_ref[0])).astype(BF16)


def _attention_b(qb, kb, vb, km, vm, zb, tq, tk):
    bn, s, _ = qb.shape
    grid = (bn, N_KV, s // tq)
    qspec = pl.BlockSpec((1, tq, GROUP * HEAD_DIM), lambda b, g, i: (b, i, g))
    kspec = pl.BlockSpec((1, 1, s, HEAD_DIM), lambda b, g, i: (b, g, 0, 0))
    meta = pl.BlockSpec((1, 1, N_META, HEAD_DIM), lambda b, g, i: (0, g, 0, 0))
    return pl.pallas_call(
        functools.partial(_attn_b_kernel, tq=tq, tk=tk),
        grid=grid,
        in_specs=[qspec, kspec, kspec, meta, meta, qspec],
        out_specs=qspec,
        out_shape=jax.ShapeDtypeStruct((bn, s, WIDTH), BF16),
        compiler_params=pltpu.CompilerParams(
            dimension_semantics=("arbitrary", "arbitrary", "arbitrary"),
            vmem_limit_bytes=VMEM_LIMIT),
        name="attn_b",
    )(qb, kb, vb, km, vm, zb)


def _out_kernel(ya_ref, yb_ref, g_ref, x_ref, wa_ref, wb_ref, wo_ref, ln_ref, o_ref):
    ua = jnp.dot(ya_ref[0], wa_ref[...], preferred_element_type=F32)
    ub = jnp.dot(yb_ref[0], wb_ref[...], preferred_element_type=F32)
    merged = (jax.nn.sigmoid(g_ref[0, :, :D_MODEL]) * ua
              + jax.nn.sigmoid(g_ref[0, :, D_MODEL:]) * ub)
    out = jnp.dot(merged.astype(BF16), wo_ref[...], preferred_element_type=F32)
    y = ALPHA * x_ref[0] + out
    mu = jnp.mean(y, axis=1, keepdims=True)
    d = y - mu
    var = jnp.mean(d * d, axis=1, keepdims=True)
    o_ref[0] = d * lax.rsqrt(var + LN_EPS) * ln_ref[0:1, :] + ln_ref[1:2, :]


def _output(ya, yb, gates, x, wa_bf, wb_bf, wo_bf, ln, tm):
    bn, s, _ = x.shape
    grid = (bn, s // tm)
    tok = lambda n: pl.BlockSpec((1, tm, n), lambda b, i: (b, i, 0))
    const = lambda shape: pl.BlockSpec(shape, lambda b, i: (0,) * len(shape))
    return pl.pallas_call(
        _out_kernel,
        grid=grid,
        in_specs=[tok(WIDTH), tok(WIDTH), tok(2 * D_MODEL), tok(D_MODEL),
                  const((WIDTH, D_MODEL)), const((WIDTH, D_MODEL)), const((D_MODEL, D_MODEL)),
                  const((2, D_MODEL))],
        out_specs=tok(D_MODEL),
        out_shape=jax.ShapeDtypeStruct((bn, s, D_MODEL), F32),
        compiler_params=pltpu.CompilerParams(
            dimension_semantics=("arbitrary", "arbitrary"), vmem_limit_bytes=VMEM_LIMIT),
        name="out",
    )(ya, yb, gates, x, wa_bf, wb_bf, wo_bf, ln)


def _encode(x, meta_kv, w_in_bf, tables, gains, sink_rows, wa_bf, wb_bf, wo_bf, ln):
    kma, vma, kmb, vmb = meta_kv
    qa, ka, va, za, qb, kb, vb, zb, gates = _project(x, w_in_bf, tables, gains, tm=256)
    ya = _attention_a(qa, ka, va, kma, vma, sink_rows, za)
    yb = _attention_b(qb, kb, vb, kmb, vmb, zb, tq=128, tk=512)
    return _output(ya, yb, gates, x, wa_bf, wb_bf, wo_bf, ln, tm=256)


def kernel(x_prompt, x_sample, meta_tokens, w_in, attn_a_sink, q_norm_b, k_norm_b,
           w_branch_a, w_branch_b, w_out, ln_gain, ln_bias):
    assert w_in.shape[0] == DEPTH
    s = x_prompt.shape[1]
    assert x_sample.shape[1] == s and s % GRID_W == 0
    w_in_bf = w_in[0].astype(BF16)
    wa_bf = w_branch_a[0].astype(BF16)
    wb_bf = w_branch_b[0].astype(BF16)
    wo_bf = w_out[0].astype(BF16)
    gains = jnp.stack([jnp.tile(q_norm_b[0], 2), jnp.tile(k_norm_b[0], 2)]).astype(F32)
    ln = jnp.stack([ln_gain[0], ln_bias[0]]).astype(F32)
    sink_rows = jnp.repeat(attn_a_sink[0].astype(F32).reshape(N_KV, GROUP), BLOCK, axis=1)[..., None]

    meta_pos = jnp.arange(N_META) - N_META
    rows = s // GRID_W
    tables = _rope_tables(jnp.arange(s) + N_META,
                          jnp.repeat(jnp.arange(rows), GRID_W),
                          jnp.tile(jnp.arange(GRID_W), rows))
    meta_tables = _rope_tables(jnp.arange(N_META), meta_pos, meta_pos)

    mp = _project(meta_tokens[None].astype(F32), w_in_bf, meta_tables, gains, tm=N_META)
    meta_kv = (mp[1], mp[2], mp[5], mp[6])

    enc = functools.partial(_encode, meta_kv=meta_kv, w_in_bf=w_in_bf, tables=tables, gains=gains,
                            sink_rows=sink_rows, wa_bf=wa_bf, wb_bf=wb_bf, wo_bf=wo_bf, ln=ln)
    return (enc(x_prompt), enc(x_sample))
```

```python
import functools

import numpy as np
import jax
import jax.numpy as jnp
from jax import lax
from jax.experimental import pallas as pl
from jax.experimental.pallas import tpu as pltpu

D_MODEL = 1024
HEAD_DIM = 64
N_HEADS = 8
N_KV = 2
GROUP = N_HEADS // N_KV
WIDTH = N_HEADS * HEAD_DIM
KV_WIDTH = N_KV * HEAD_DIM
WINDOW = 128
BLOCK = 128
N_META = 16
GRID_W = 64
ROPE_THETA = 10000.0
LN_EPS = 1e-5
RMS_EPS = 1e-6
NEG_INF = -1e30
DEPTH = 1
ALPHA = (2.0 * DEPTH) ** 0.25
SCALE = HEAD_DIM ** -0.5
LOG2E = 1.4426950408889634
MAX_ROWS = 8

LANES = 128
VT_ROWS = HEAD_DIM + 16
VMEM_LIMIT = 56 * 1024 * 1024

OFF_QA = 0
OFF_KA = OFF_QA + WIDTH
OFF_VA = OFF_KA + KV_WIDTH
OFF_ZA = OFF_VA + KV_WIDTH
OFF_QB = OFF_ZA + WIDTH
OFF_KB = OFF_QB + WIDTH
OFF_VB = OFF_KB + KV_WIDTH
OFF_ZB = OFF_VB + KV_WIDTH
OFF_G = OFF_ZB + WIDTH
D_IN = OFF_G + 2 * D_MODEL

BF16 = jnp.bfloat16
F32 = jnp.float32


def _rope_angles(pos, dim):
    inv = ROPE_THETA ** (-jnp.arange(0, dim, 2, dtype=F32) / dim)
    return pos.astype(F32)[:, None] * inv[None, :]


def _rope_tables(pos_a, row, col):
    lane = np.arange(LANES)
    ang_a = _rope_angles(pos_a, HEAD_DIM)
    cos_a = jnp.tile(jnp.cos(ang_a), (1, 4))
    sin_a = jnp.tile(jnp.sin(ang_a), (1, 4))
    lo_a = jnp.asarray((lane % 64) < 32)[None, :]
    h = HEAD_DIM // 2
    ang_r = _rope_angles(row, h)
    ang_c = _rope_angles(col, h)
    cos_b = jnp.tile(jnp.concatenate([jnp.cos(ang_r)] * 2 + [jnp.cos(ang_c)] * 2, axis=1), (1, 2))
    sin_b = jnp.tile(jnp.concatenate([jnp.sin(ang_r)] * 2 + [jnp.sin(ang_c)] * 2, axis=1), (1, 2))
    lo_b = jnp.asarray((lane % 32) < 16)[None, :]
    zero = jnp.zeros_like(sin_a)
    return jnp.concatenate([
        cos_a, jnp.where(lo_a, -sin_a, zero), jnp.where(lo_a, zero, sin_a),
        cos_b, jnp.where(lo_b, -sin_b, zero), jnp.where(lo_b, zero, sin_b)], axis=1)


def _rope(x, cos, sin_lo, sin_hi, half):
    return (x * cos + pltpu.roll(x, LANES - half, 1) * sin_lo
            + pltpu.roll(x, half, 1) * sin_hi)


def _head_rms(x, gain):
    xx = x * x
    s0 = jnp.sum(xx[:, :HEAD_DIM], axis=1, keepdims=True)
    s1 = jnp.sum(xx[:, HEAD_DIM:], axis=1, keepdims=True)
    lane = lax.broadcasted_iota(jnp.int32, x.shape, 1)
    ms = jnp.where(lane < HEAD_DIM, s0, s1) * (1.0 / HEAD_DIM)
    return x * lax.rsqrt(ms + RMS_EPS) * gain


def _proj_kernel(x_ref, w_ref, tab_ref, gain_ref,
                 qa_ref, ka_ref, va_ref, za_ref, qb_ref, kb_ref, vb_ref, zb_ref, g_ref):
    xb = x_ref[0].astype(BF16)

    def mm(off, n):
        return jnp.dot(xb, w_ref[:, off:off + n], preferred_element_type=F32)

    def tab(i):
        return tab_ref[:, i * LANES:(i + 1) * LANES]

    cos_a, slo_a, shi_a, cos_b, slo_b, shi_b = (tab(i) for i in range(6))
    gq = gain_ref[0:1, :]
    gk = gain_ref[1:2, :]

    def split_kv(ref, x):
        ref[0, 0] = x[:, :HEAD_DIM].astype(BF16)
        ref[0, 1] = x[:, HEAD_DIM:].astype(BF16)

    qa = mm(OFF_QA, WIDTH)
    for c in range(WIDTH // LANES):
        xc = qa[:, c * LANES:(c + 1) * LANES]
        qa_ref[0, :, c * LANES:(c + 1) * LANES] = (
            _rope(xc, cos_a, slo_a, shi_a, HEAD_DIM // 2) * SCALE).astype(BF16)
    kva = mm(OFF_KA, 2 * KV_WIDTH)
    split_kv(ka_ref, _rope(kva[:, :KV_WIDTH], cos_a, slo_a, shi_a, HEAD_DIM // 2))
    split_kv(va_ref, kva[:, KV_WIDTH:])
    za_ref[0] = mm(OFF_ZA, WIDTH)

    qb = mm(OFF_QB, WIDTH)
    for c in range(WIDTH // LANES):
        xc = _head_rms(qb[:, c * LANES:(c + 1) * LANES], gq)
        xt = (_rope(xc, cos_b, slo_b, shi_b, HEAD_DIM // 4) * (SCALE * LOG2E)).T.astype(BF16)
        qb_ref[0, 2 * c] = xt[:HEAD_DIM]
        qb_ref[0, 2 * c + 1] = xt[HEAD_DIM:]
    kvb = mm(OFF_KB, 2 * KV_WIDTH)
    split_kv(kb_ref, _rope(_head_rms(kvb[:, :KV_WIDTH], gk), cos_b, slo_b, shi_b, HEAD_DIM // 4))
    vt = kvb[:, KV_WIDTH:].T.astype(BF16)
    tm = vt.shape[1]
    row = lax.broadcasted_iota(jnp.int32, (VT_ROWS - HEAD_DIM, tm), 0)
    tail = jnp.where(row == 0, 1.0, 0.0).astype(BF16)
    for h in range(N_KV):
        vb_ref[0, h, :HEAD_DIM, :] = vt[h * HEAD_DIM:(h + 1) * HEAD_DIM]
        vb_ref[0, h, HEAD_DIM:, :] = tail
    zb_ref[0] = mm(OFF_ZB, WIDTH)

    for c in range(2 * D_MODEL // 512):
        g_ref[0, :, c * 512:(c + 1) * 512] = mm(OFF_G + c * 512, 512)


def _project(x, w_in_bf, tables, gains, tm):
    bn, s, _ = x.shape
    grid = (s // tm, bn)
    tok = lambda n: pl.BlockSpec((1, tm, n), lambda i, b: (b, i, 0))
    kv = pl.BlockSpec((1, N_KV, tm, HEAD_DIM), lambda i, b: (b, 0, i, 0))
    kvt = pl.BlockSpec((1, N_KV, VT_ROWS, tm), lambda i, b: (b, 0, 0, i))
    qt = pl.BlockSpec((1, N_HEADS, HEAD_DIM, tm), lambda i, b: (b, 0, 0, i))
    const = lambda shape: pl.BlockSpec(shape, lambda i, b: (0,) * len(shape))
    out_shape = (
        jax.ShapeDtypeStruct((bn, s, WIDTH), BF16),
        jax.ShapeDtypeStruct((bn, N_KV, s, HEAD_DIM), BF16),
        jax.ShapeDtypeStruct((bn, N_KV, s, HEAD_DIM), BF16),
        jax.ShapeDtypeStruct((bn, s, WIDTH), F32),
        jax.ShapeDtypeStruct((bn, N_HEADS, HEAD_DIM, s), BF16),
        jax.ShapeDtypeStruct((bn, N_KV, s, HEAD_DIM), BF16),
        jax.ShapeDtypeStruct((bn, N_KV, VT_ROWS, s), BF16),
        jax.ShapeDtypeStruct((bn, s, WIDTH), F32),
        jax.ShapeDtypeStruct((bn, s, 2 * D_MODEL), F32),
    )
    return pl.pallas_call(
        _proj_kernel,
        grid=grid,
        in_specs=[tok(D_MODEL), const((D_MODEL, D_IN)),
                  pl.BlockSpec((tm, 6 * LANES), lambda i, b: (i, 0)),
                  const((2, LANES))],
        out_specs=(tok(WIDTH), kv, kv, tok(WIDTH), qt, kv, kvt, tok(WIDTH),
                   tok(2 * D_MODEL)),
        out_shape=out_shape,
        compiler_params=pltpu.CompilerParams(
            dimension_semantics=("arbitrary", "arbitrary"), vmem_limit_bytes=VMEM_LIMIT),
        name="proj",
    )(x, w_in_bf, tables, gains)


def _stack_heads(q):
    return jnp.concatenate([q[:, h * HEAD_DIM:(h + 1) * HEAD_DIM] for h in range(GROUP)], axis=0)


def _unstack_heads(o, tq):
    return jnp.concatenate([o[h * tq:(h + 1) * tq] for h in range(GROUP)], axis=1)


def _qk(q, k):
    return lax.dot_general(q, k, (((1,), (1,)), ((), ())), preferred_element_type=F32)


def _silu(z):
    return z * jax.nn.sigmoid(z)


def _attn_a_kernel(q_ref, kp_ref, kc_ref, kn_ref, vp_ref, vc_ref, vn_ref, km_ref, vm_ref,
                   sink_ref, z_ref, o_ref, *, nb):
    i = pl.program_id(2)
    q4 = _stack_heads(q_ref[0])
    kb = jnp.concatenate([kp_ref[0, 0], kc_ref[0, 0], kn_ref[0, 0]], axis=0)
    vb = jnp.concatenate([vp_ref[0, 0], vc_ref[0, 0], vn_ref[0, 0]], axis=0)
    s = _qk(q4, kb)
    r = lax.broadcasted_iota(jnp.int32, (BLOCK, 3 * BLOCK), 0)
    c = lax.broadcasted_iota(jnp.int32, (BLOCK, 3 * BLOCK), 1)
    rel = c - BLOCK - r
    blk = i - 1 + c // BLOCK
    valid = (jnp.abs(rel) <= WINDOW) & (blk >= 0) & (blk < nb)
    s = jnp.where(valid[None], s.reshape(GROUP, BLOCK, 3 * BLOCK), NEG_INF)
    s = s.reshape(GROUP * BLOCK, 3 * BLOCK)
    sm = _qk(q4, km_ref[0, 0])
    sink = sink_ref[0]
    m = jnp.maximum(jnp.maximum(jnp.max(s, axis=1, keepdims=True),
                                jnp.max(sm, axis=1, keepdims=True)), sink)
    p = jnp.exp(s - m)
    pm = jnp.exp(sm - m)
    l = (jnp.sum(p, axis=1, keepdims=True) + jnp.sum(pm, axis=1, keepdims=True)
         + jnp.exp(sink - m))
    o = (jnp.dot(p.astype(BF16), vb, preferred_element_type=F32)
         + jnp.dot(pm.astype(BF16), vm_ref[0, 0], preferred_element_type=F32))
    o = _unstack_heads(o / l, BLOCK)
    o_ref[0] = (o * _silu(z_ref[0])).astype(BF16)


def _attention_a(qa, ka, va, km, vm, sink_rows, za):
    bn, s, _ = qa.shape
    nb = s // BLOCK
    grid = (bn, N_KV, nb)
    qspec = pl.BlockSpec((1, BLOCK, GROUP * HEAD_DIM), lambda b, g, i: (b, i, g))
    kprev = pl.BlockSpec((1, 1, BLOCK, HEAD_DIM), lambda b, g, i: (b, g, jnp.maximum(i - 1, 0), 0))
    kcur = pl.BlockSpec((1, 1, BLOCK, HEAD_DIM), lambda b, g, i: (b, g, i, 0))
    knext = pl.BlockSpec((1, 1, BLOCK, HEAD_DIM),
                         lambda b, g, i: (b, g, jnp.minimum(i + 1, nb - 1), 0))
    meta = pl.BlockSpec((1, 1, N_META, HEAD_DIM), lambda b, g, i: (0, g, 0, 0))
    sink = pl.BlockSpec((1, GROUP * BLOCK, 1), lambda b, g, i: (g, 0, 0))
    return pl.pallas_call(
        functools.partial(_attn_a_kernel, nb=nb),
        grid=grid,
        in_specs=[qspec, kprev, kcur, knext, kprev, kcur, knext, meta, meta, sink, qspec],
        out_specs=qspec,
        out_shape=jax.ShapeDtypeStruct((bn, s, WIDTH), BF16),
        compiler_params=pltpu.CompilerParams(
            dimension_semantics=("arbitrary", "arbitrary", "arbitrary"),
            vmem_limit_bytes=VMEM_LIMIT),
        name="attn_a",
    )(qa, ka, ka, ka, va, va, va, km, vm, sink_rows, za)


def _attn_b_kernel(q_ref, k_ref, v_ref, km_ref, vm_ref, z_ref, o_ref, s0_ref, s1_ref,
                   *, tq, tk, sub):
    mq = GROUP * tq
    q4t = jnp.concatenate([q_ref[0, h] for h in range(GROUP)], axis=1)
    n_chunks = k_ref.shape[2] // tk
    s_refs = (s0_ref, s1_ref)

    def step(c, slot, m, part, acc, do_scores=True):
        m_new = jnp.maximum(m, jnp.max(part, axis=0, keepdims=True))
        part_next, pv = None, None
        for u in range(tk // sub):
            rows = slice(u * sub, (u + 1) * sub)
            if do_scores:
                start = pl.multiple_of((c + 1) * tk + u * sub, sub)
                blk = jnp.dot(k_ref[0, 0, pl.ds(start, sub), :], q4t,
                              preferred_element_type=F32)
                s_refs[1 - slot][rows, :] = blk
                bm = jnp.max(blk.reshape(sub // MAX_ROWS, MAX_ROWS, mq), axis=0)
                part_next = bm if part_next is None else jnp.maximum(part_next, bm)
            p_t = jnp.exp2(s_refs[slot][rows, :] - m_new).astype(BF16)
            start = pl.multiple_of(c * tk + u * sub, sub)
            d = jnp.dot(v_ref[0, 0, :, pl.ds(start, sub)], p_t, preferred_element_type=F32)
            pv = d if pv is None else pv + d
        return m_new, part_next, jnp.exp2(m - m_new) * acc + pv

    s_t = jnp.dot(km_ref[0, 0], q4t, preferred_element_type=F32)
    m = jnp.max(s_t, axis=0, keepdims=True)
    p_t = jnp.exp2(s_t - m).astype(BF16)
    acc = jnp.dot(vm_ref[0, 0], p_t, preferred_element_type=F32)

    s0 = jnp.dot(k_ref[0, 0, 0:tk, :], q4t, preferred_element_type=F32)
    s0_ref[...] = s0
    part = jnp.max(s0.reshape(tk // MAX_ROWS, MAX_ROWS, mq), axis=0)

    def body(c, carry):
        return lax.cond(c % 2 == 0,
                        lambda cr: step(c, 0, *cr),
                        lambda cr: step(c, 1, *cr), carry)

    m, part, acc = lax.fori_loop(0, n_chunks - 1, body, (m, part, acc))
    _, _, acc = step(n_chunks - 1, (n_chunks - 1) % 2, m, part, acc, do_scores=False)
    o_t = acc[:HEAD_DIM] / acc[HEAD_DIM:HEAD_DIM + 1]
    o = jnp.concatenate([o_t[:, h * tq:(h + 1) * tq] for h in range(GROUP)], axis=0).T
    o_ref[0] = (o * _silu(z_ref[0])).astype(BF16)


def _attention_b(qb, kb, vb, km, vm, zb, tq, tk, sub):
    bn, _, s = vb.shape[0], vb.shape[1], vb.shape[3]
    grid = (bn, N_KV, s // tq)
    qtspec = pl.BlockSpec((1, GROUP, HEAD_DIM, tq), lambda b, g, i: (b, g, 0, i))
    qspec = pl.BlockSpec((1, tq, GROUP * HEAD_DIM), lambda b, g, i: (b, i, g))
    kspec = pl.BlockSpec((1, 1, s, HEAD_DIM), lambda b, g, i: (b, g, 0, 0))
    vspec = pl.BlockSpec((1, 1, VT_ROWS, s), lambda b, g, i: (b, g, 0, 0))
    kmeta = pl.BlockSpec((1, 1, N_META, HEAD_DIM), lambda b, g, i: (0, g, 0, 0))
    vmeta = pl.BlockSpec((1, 1, VT_ROWS, N_META), lambda b, g, i: (0, g, 0, 0))
    return pl.pallas_call(
        functools.partial(_attn_b_kernel, tq=tq, tk=tk, sub=sub),
        grid=grid,
        in_specs=[qtspec, kspec, vspec, kmeta, vmeta, qspec],
        out_specs=qspec,
        out_shape=jax.ShapeDtypeStruct((bn, s, WIDTH), BF16),
        scratch_shapes=[pltpu.VMEM((tk, GROUP * tq), F32), pltpu.VMEM((tk, GROUP * tq), F32)],
        compiler_params=pltpu.CompilerParams(
            dimension_semantics=("arbitrary", "arbitrary", "arbitrary"),
            vmem_limit_bytes=VMEM_LIMIT),
        name="attn_b",
    )(qb, kb, vb, km, vm, zb)


def _out_kernel(ya_ref, yb_ref, g_ref, x_ref, wa_ref, wb_ref, wo_ref, ln_ref, o_ref):
    ua = jnp.dot(ya_ref[0], wa_ref[...], preferred_element_type=F32)
    ub = jnp.dot(yb_ref[0], wb_ref[...], preferred_element_type=F32)
    merged = (jax.nn.sigmoid(g_ref[0, :, :D_MODEL]) * ua
              + jax.nn.sigmoid(g_ref[0, :, D_MODEL:]) * ub)
    out = jnp.dot(merged.astype(BF16), wo_ref[...], preferred_element_type=F32)
    y = ALPHA * x_ref[0] + out
    mu = jnp.mean(y, axis=1, keepdims=True)
    d = y - mu
    var = jnp.mean(d * d, axis=1, keepdims=True)
    o_ref[0] = d * lax.rsqrt(var + LN_EPS) * ln_ref[0:1, :] + ln_ref[1:2, :]


def _output(ya, yb, gates, x, wa_bf, wb_bf, wo_bf, ln, tm):
    bn, s, _ = x.shape
    grid = (bn, s // tm)
    tok = lambda n: pl.BlockSpec((1, tm, n), lambda b, i: (b, i, 0))
    const = lambda shape: pl.BlockSpec(shape, lambda b, i: (0,) * len(shape))
    return pl.pallas_call(
        _out_kernel,
        grid=grid,
        in_specs=[tok(WIDTH), tok(WIDTH), tok(2 * D_MODEL), tok(D_MODEL),
                  const((WIDTH, D_MODEL)), const((WIDTH, D_MODEL)), const((D_MODEL, D_MODEL)),
                  const((2, D_MODEL))],
        out_specs=tok(D_MODEL),
        out_shape=jax.ShapeDtypeStruct((bn, s, D_MODEL), F32),
        compiler_params=pltpu.CompilerParams(
            dimension_semantics=("arbitrary", "arbitrary"), vmem_limit_bytes=VMEM_LIMIT),
        name="out",
    )(ya, yb, gates, x, wa_bf, wb_bf, wo_bf, ln)


def _encode(x, meta_kv, w_in_bf, tables, gains, sink_rows, wa_bf, wb_bf, wo_bf, ln):
    kma, vma, kmb, vmb = meta_kv
    qa, ka, va, za, qb, kb, vb, zb, gates = _project(x, w_in_bf, tables, gains, tm=256)
    ya = _attention_a(qa, ka, va, kma, vma, sink_rows, za)
    yb = _attention_b(qb, kb, vb, kmb, vmb, zb, tq=128, tk=2048, sub=256)
    return _output(ya, yb, gates, x, wa_bf, wb_bf, wo_bf, ln, tm=256)


def kernel(x_prompt, x_sample, meta_tokens, w_in, attn_a_sink, q_norm_b, k_norm_b,
           w_branch_a, w_branch_b, w_out, ln_gain, ln_bias):
    assert w_in.shape[0] == DEPTH
    s = x_prompt.shape[1]
    assert x_sample.shape[1] == s and s % GRID_W == 0
    w_in_bf = w_in[0].astype(BF16)
    wa_bf = w_branch_a[0].astype(BF16)
    wb_bf = w_branch_b[0].astype(BF16)
    wo_bf = w_out[0].astype(BF16)
    gains = jnp.stack([jnp.tile(q_norm_b[0], 2), jnp.tile(k_norm_b[0], 2)]).astype(F32)
    ln = jnp.stack([ln_gain[0], ln_bias[0]]).astype(F32)
    sink_rows = jnp.repeat(attn_a_sink[0].astype(F32).reshape(N_KV, GROUP), BLOCK, axis=1)[..., None]

    meta_pos = jnp.arange(N_META) - N_META
    rows = s // GRID_W
    tables = _rope_tables(jnp.arange(s) + N_META,
                          jnp.repeat(jnp.arange(rows), GRID_W),
                          jnp.tile(jnp.arange(GRID_W), rows))
    pad = BLOCK - N_META
    meta_pos_pad = jnp.pad(meta_pos, (0, pad))
    meta_tables = _rope_tables(jnp.arange(BLOCK), meta_pos_pad, meta_pos_pad)
    meta_x = jnp.pad(meta_tokens.astype(F32), ((0, pad), (0, 0)))[None]
    mp = _project(meta_x, w_in_bf, meta_tables, gains, tm=BLOCK)
    meta_kv = (mp[1][:, :, :N_META], mp[2][:, :, :N_META], mp[5][:, :, :N_META],
               mp[6][:, :, :, :N_META])

    enc = functools.partial(_encode, meta_kv=meta_kv, w_in_bf=w_in_bf, tables=tables, gains=gains,
                            sink_rows=sink_rows, wa_bf=wa_bf, wb_bf=wb_bf, wo_bf=wo_bf, ln=ln)
    return (enc(x_prompt), enc(x_sample))
```

```python
import functools

import numpy as np
import jax
import jax.numpy as jnp
from jax import lax
from jax.experimental import pallas as pl
from jax.experimental.pallas import tpu as pltpu

D_MODEL = 1024
HEAD_DIM = 64
N_HEADS = 8
N_KV = 2
GROUP = N_HEADS // N_KV
WIDTH = N_HEADS * HEAD_DIM
KV_WIDTH = N_KV * HEAD_DIM
WINDOW = 128
BLOCK = 128
N_META = 16
GRID_W = 64
ROPE_THETA = 10000.0
LN_EPS = 1e-5
RMS_EPS = 1e-6
NEG_INF = -1e30
DEPTH = 1
ALPHA = (2.0 * DEPTH) ** 0.25
SCALE = HEAD_DIM ** -0.5
LOG2E = 1.4426950408889634
MAX_ROWS = 8

LANES = 128
VT_ROWS = HEAD_DIM + 16
VMEM_LIMIT = 56 * 1024 * 1024

OFF_QA = 0
OFF_KA = OFF_QA + WIDTH
OFF_VA = OFF_KA + KV_WIDTH
OFF_ZA = OFF_VA + KV_WIDTH
OFF_QB = OFF_ZA + WIDTH
OFF_KB = OFF_QB + WIDTH
OFF_VB = OFF_KB + KV_WIDTH
OFF_ZB = OFF_VB + KV_WIDTH
OFF_G = OFF_ZB + WIDTH
D_IN = OFF_G + 2 * D_MODEL

BF16 = jnp.bfloat16
F32 = jnp.float32


def _rope_angles(pos, dim):
    inv = ROPE_THETA ** (-jnp.arange(0, dim, 2, dtype=F32) / dim)
    return pos.astype(F32)[:, None] * inv[None, :]


def _rope_tables(pos_a, row, col):
    lane = np.arange(LANES)
    ang_a = _rope_angles(pos_a, HEAD_DIM)
    cos_a = jnp.tile(jnp.cos(ang_a), (1, 4))
    sin_a = jnp.tile(jnp.sin(ang_a), (1, 4))
    lo_a = jnp.asarray((lane % 64) < 32)[None, :]
    h = HEAD_DIM // 2
    ang_r = _rope_angles(row, h)
    ang_c = _rope_angles(col, h)
    cos_b = jnp.tile(jnp.concatenate([jnp.cos(ang_r)] * 2 + [jnp.cos(ang_c)] * 2, axis=1), (1, 2))
    sin_b = jnp.tile(jnp.concatenate([jnp.sin(ang_r)] * 2 + [jnp.sin(ang_c)] * 2, axis=1), (1, 2))
    lo_b = jnp.asarray((lane % 32) < 16)[None, :]
    zero = jnp.zeros_like(sin_a)
    return jnp.concatenate([
        cos_a, jnp.where(lo_a, -sin_a, zero), jnp.where(lo_a, zero, sin_a),
        cos_b, jnp.where(lo_b, -sin_b, zero), jnp.where(lo_b, zero, sin_b)], axis=1)


def _rope(x, cos, sin_lo, sin_hi, half):
    return (x * cos + pltpu.roll(x, LANES - half, 1) * sin_lo
            + pltpu.roll(x, half, 1) * sin_hi)


def _head_rms(x, gain):
    xx = x * x
    s0 = jnp.sum(xx[:, :HEAD_DIM], axis=1, keepdims=True)
    s1 = jnp.sum(xx[:, HEAD_DIM:], axis=1, keepdims=True)
    lane = lax.broadcasted_iota(jnp.int32, x.shape, 1)
    ms = jnp.where(lane < HEAD_DIM, s0, s1) * (1.0 / HEAD_DIM)
    return x * lax.rsqrt(ms + RMS_EPS) * gain


def _proj_kernel(x_ref, w_ref, tab_ref, gain_ref,
                 qa_ref, ka_ref, va_ref, za_ref, qb_ref, kb_ref, vb_ref, zb_ref, g_ref):
    xb = x_ref[0].astype(BF16)

    def mm(off, n):
        return jnp.dot(xb, w_ref[:, off:off + n], preferred_element_type=F32)

    def tab(i):
        return tab_ref[:, i * LANES:(i + 1) * LANES]

    cos_a, slo_a, shi_a, cos_b, slo_b, shi_b = (tab(i) for i in range(6))
    gq = gain_ref[0:1, :]
    gk = gain_ref[1:2, :]

    tm = xb.shape[0]
    row = lax.broadcasted_iota(jnp.int32, (VT_ROWS - HEAD_DIM, tm), 0)
    ones_row = jnp.where(row == 0, 1.0, 0.0).astype(BF16)

    def put_q(ref, c, x):
        xt = (x * (SCALE * LOG2E)).T.astype(BF16)
        ref[0, 2 * c] = xt[:HEAD_DIM]
        ref[0, 2 * c + 1] = xt[HEAD_DIM:]

    def put_k(ref, x):
        ref[0, 0] = x[:, :HEAD_DIM].astype(BF16)
        ref[0, 1] = x[:, HEAD_DIM:].astype(BF16)

    def put_v(ref, x):
        xt = x.T.astype(BF16)
        for h in range(N_KV):
            ref[0, h, :HEAD_DIM, :] = xt[h * HEAD_DIM:(h + 1) * HEAD_DIM]
            ref[0, h, HEAD_DIM:, :] = ones_row

    qa = mm(OFF_QA, WIDTH)
    for c in range(WIDTH // LANES):
        put_q(qa_ref, c, _rope(qa[:, c * LANES:(c + 1) * LANES], cos_a, slo_a, shi_a, HEAD_DIM // 2))
    kva = mm(OFF_KA, 2 * KV_WIDTH)
    put_k(ka_ref, _rope(kva[:, :KV_WIDTH], cos_a, slo_a, shi_a, HEAD_DIM // 2))
    put_v(va_ref, kva[:, KV_WIDTH:])
    za_ref[0] = mm(OFF_ZA, WIDTH)

    qb = mm(OFF_QB, WIDTH)
    for c in range(WIDTH // LANES):
        xc = _head_rms(qb[:, c * LANES:(c + 1) * LANES], gq)
        put_q(qb_ref, c, _rope(xc, cos_b, slo_b, shi_b, HEAD_DIM // 4))
    kvb = mm(OFF_KB, 2 * KV_WIDTH)
    put_k(kb_ref, _rope(_head_rms(kvb[:, :KV_WIDTH], gk), cos_b, slo_b, shi_b, HEAD_DIM // 4))
    put_v(vb_ref, kvb[:, KV_WIDTH:])
    zb_ref[0] = mm(OFF_ZB, WIDTH)

    for c in range(2 * D_MODEL // 512):
        g_ref[0, :, c * 512:(c + 1) * 512] = mm(OFF_G + c * 512, 512)


def _project(x, w_in_bf, tables, gains, tm):
    bn, s, _ = x.shape
    grid = (s // tm, bn)
    tok = lambda n: pl.BlockSpec((1, tm, n), lambda i, b: (b, i, 0))
    kv = pl.BlockSpec((1, N_KV, tm, HEAD_DIM), lambda i, b: (b, 0, i, 0))
    kvt = pl.BlockSpec((1, N_KV, VT_ROWS, tm), lambda i, b: (b, 0, 0, i))
    qt = pl.BlockSpec((1, N_HEADS, HEAD_DIM, tm), lambda i, b: (b, 0, 0, i))
    const = lambda shape: pl.BlockSpec(shape, lambda i, b: (0,) * len(shape),
                                       pipeline_mode=pl.Buffered(1))
    q_shape = jax.ShapeDtypeStruct((bn, N_HEADS, HEAD_DIM, s), BF16)
    k_shape = jax.ShapeDtypeStruct((bn, N_KV, s, HEAD_DIM), BF16)
    v_shape = jax.ShapeDtypeStruct((bn, N_KV, VT_ROWS, s), BF16)
    z_shape = jax.ShapeDtypeStruct((bn, s, WIDTH), F32)
    out_shape = (q_shape, k_shape, v_shape, z_shape,
                 q_shape, k_shape, v_shape, z_shape,
                 jax.ShapeDtypeStruct((bn, s, 2 * D_MODEL), F32))
    return pl.pallas_call(
        _proj_kernel,
        grid=grid,
        in_specs=[tok(D_MODEL), const((D_MODEL, D_IN)),
                  pl.BlockSpec((tm, 6 * LANES), lambda i, b: (i, 0)),
                  const((2, LANES))],
        out_specs=(qt, kv, kvt, tok(WIDTH), qt, kv, kvt, tok(WIDTH), tok(2 * D_MODEL)),
        out_shape=out_shape,
        compiler_params=pltpu.CompilerParams(
            dimension_semantics=("arbitrary", "arbitrary"), vmem_limit_bytes=VMEM_LIMIT),
        name="proj",
    )(x, w_in_bf, tables, gains)


def _stack_heads(q):
    return jnp.concatenate([q[:, h * HEAD_DIM:(h + 1) * HEAD_DIM] for h in range(GROUP)], axis=0)


def _unstack_heads(o, tq):
    return jnp.concatenate([o[h * tq:(h + 1) * tq] for h in range(GROUP)], axis=1)


def _qk(q, k):
    return lax.dot_general(q, k, (((1,), (1,)), ((), ())), preferred_element_type=F32)


def _silu(z):
    return z * jax.nn.sigmoid(z)


def _attn_a_kernel(q_ref, kp_ref, kc_ref, kn_ref, vp_ref, vc_ref, vn_ref, km_ref, vm_ref,
                   sink_ref, z_ref, o_ref, *, nblk):
    i = pl.program_id(2)
    last = pl.num_programs(2) - 1
    kwin = jnp.concatenate([kp_ref[0, 0], kc_ref[0, 0], kn_ref[0, 0]], axis=0)
    vwin = jnp.concatenate([vp_ref[0, 0], vc_ref[0, 0], vn_ref[0, 0]], axis=1)
    r = lax.broadcasted_iota(jnp.int32, (3 * BLOCK, BLOCK), 0)
    c = lax.broadcasted_iota(jnp.int32, (3 * BLOCK, BLOCK), 1)
    band = jnp.abs(r - BLOCK - c) <= WINDOW
    sink = sink_ref[0]
    for j in range(nblk):
        q4t = jnp.concatenate([q_ref[0, h, :, j * BLOCK:(j + 1) * BLOCK] for h in range(GROUP)],
                              axis=1)
        valid = band
        if j == 0:
            valid = valid & ((r >= BLOCK) | (i > 0))
        if j == nblk - 1:
            valid = valid & ((r < 2 * BLOCK) | (i < last))
        s = jnp.dot(kwin[j * BLOCK:(j + 3) * BLOCK], q4t, preferred_element_type=F32)
        s = jnp.concatenate(
            [jnp.where(valid, s[:, h * BLOCK:(h + 1) * BLOCK], NEG_INF) for h in range(GROUP)], axis=1)
        sm = jnp.dot(km_ref[0, 0], q4t, preferred_element_type=F32)
        m = jnp.maximum(jnp.maximum(jnp.max(s, axis=0, keepdims=True),
                                    jnp.max(sm, axis=0, keepdims=True)), sink)
        p = jnp.exp2(s - m).astype(BF16)
        pm = jnp.exp2(sm - m).astype(BF16)
        o_t = (jnp.dot(vwin[:, j * BLOCK:(j + 3) * BLOCK], p, preferred_element_type=F32)
               + jnp.dot(vm_ref[0, 0], pm, preferred_element_type=F32))
        l = o_t[HEAD_DIM:HEAD_DIM + 1] + jnp.exp2(sink - m)
        o_t = o_t[:HEAD_DIM] / l
        o = jnp.concatenate([o_t[:, h * BLOCK:(h + 1) * BLOCK] for h in range(GROUP)], axis=0).T
        rows = slice(j * BLOCK, (j + 1) * BLOCK)
        o_ref[0, rows, :] = (o * _silu(z_ref[0, rows, :])).astype(BF16)


def _attention_a(qa, ka, va, km, vm, sink_rows, za, nblk):
    bn, s = va.shape[0], va.shape[3]
    nb = s // BLOCK
    grid = (bn, N_KV, nb // nblk)
    tq = nblk * BLOCK
    qtspec = pl.BlockSpec((1, GROUP, HEAD_DIM, tq), lambda b, g, i: (b, g, 0, i))
    zspec = pl.BlockSpec((1, tq, GROUP * HEAD_DIM), lambda b, g, i: (b, i, g))
    prev = lambda i: jnp.maximum(i * nblk - 1, 0)
    nxt = lambda i: jnp.minimum((i + 1) * nblk, nb - 1)
    kprev = pl.BlockSpec((1, 1, BLOCK, HEAD_DIM), lambda b, g, i: (b, g, prev(i), 0))
    kcur = pl.BlockSpec((1, 1, tq, HEAD_DIM), lambda b, g, i: (b, g, i, 0))
    knext = pl.BlockSpec((1, 1, BLOCK, HEAD_DIM), lambda b, g, i: (b, g, nxt(i), 0))
    vprev = pl.BlockSpec((1, 1, VT_ROWS, BLOCK), lambda b, g, i: (b, g, 0, prev(i)))
    vcur = pl.BlockSpec((1, 1, VT_ROWS, tq), lambda b, g, i: (b, g, 0, i))
    vnext = pl.BlockSpec((1, 1, VT_ROWS, BLOCK), lambda b, g, i: (b, g, 0, nxt(i)))
    kmeta = pl.BlockSpec((1, 1, N_META, HEAD_DIM), lambda b, g, i: (0, g, 0, 0))
    vmeta = pl.BlockSpec((1, 1, VT_ROWS, N_META), lambda b, g, i: (0, g, 0, 0))
    sink = pl.BlockSpec((1, 1, GROUP * BLOCK), lambda b, g, i: (g, 0, 0))
    return pl.pallas_call(
        functools.partial(_attn_a_kernel, nblk=nblk),
        grid=grid,
        in_specs=[qtspec, kprev, kcur, knext, vprev, vcur, vnext, kmeta, vmeta, sink, zspec],
        out_specs=zspec,
        out_shape=jax.ShapeDtypeStruct((bn, s, WIDTH), BF16),
        compiler_params=pltpu.CompilerParams(
            dimension_semantics=("arbitrary", "arbitrary", "arbitrary"),
            vmem_limit_bytes=VMEM_LIMIT),
        name="attn_a",
    )(qa, ka, ka, ka, va, va, va, km, vm, sink_rows, za)


def _attn_b_kernel(q_ref, k_ref, v_ref, km_ref, vm_ref, z_ref, o_ref, s0_ref, s1_ref,
                   *, tq, tk, sub):
    mq = GROUP * tq
    q4t = jnp.concatenate([q_ref[0, h] for h in range(GROUP)], axis=1)
    n_chunks = k_ref.shape[2] // tk
    s_refs = (s0_ref, s1_ref)

    def step(c, slot, m, part, acc, do_scores=True):
        m_new = jnp.maximum(m, jnp.max(part, axis=0, keepdims=True))
        part_next, pv = None, None
        for u in range(tk // sub):
            rows = slice(u * sub, (u + 1) * sub)
            if do_scores:
                start = pl.multiple_of((c + 1) * tk + u * sub, sub)
                blk = jnp.dot(k_ref[0, 0, pl.ds(start, sub), :], q4t,
                              preferred_element_type=F32)
                s_refs[1 - slot][rows, :] = blk
                bm = jnp.max(blk.reshape(sub // MAX_ROWS, MAX_ROWS, mq), axis=0)
                part_next = bm if part_next is None else jnp.maximum(part_next, bm)
            p_t = jnp.exp2(s_refs[slot][rows, :] - m_new).astype(BF16)
            start = pl.multiple_of(c * tk + u * sub, sub)
            d = jnp.dot(v_ref[0, 0, :, pl.ds(start, sub)], p_t, preferred_element_type=F32)
            pv = d if pv is None else pv + d
        return m_new, part_next, jnp.exp2(m - m_new) * acc + pv

    s_t = jnp.dot(km_ref[0, 0], q4t, preferred_element_type=F32)
    m = jnp.max(s_t, axis=0, keepdims=True)
    p_t = jnp.exp2(s_t - m).astype(BF16)
    acc = jnp.dot(vm_ref[0, 0], p_t, preferred_element_type=F32)

    s0 = jnp.dot(k_ref[0, 0, 0:tk, :], q4t, preferred_element_type=F32)
    s0_ref[...] = s0
    part = jnp.max(s0.reshape(tk // MAX_ROWS, MAX_ROWS, mq), axis=0)

    def body(c, carry):
        return lax.cond(c % 2 == 0,
                        lambda cr: step(c, 0, *cr),
                        lambda cr: step(c, 1, *cr), carry)

    m, part, acc = lax.fori_loop(0, n_chunks - 1, body, (m, part, acc))
    _, _, acc = step(n_chunks - 1, (n_chunks - 1) % 2, m, part, acc, do_scores=False)
    o_t = acc[:HEAD_DIM] / acc[HEAD_DIM:HEAD_DIM + 1]
    o = jnp.concatenate([o_t[:, h * tq:(h + 1) * tq] for h in range(GROUP)], axis=0).T
    o_ref[0] = (o * _silu(z_ref[0])).astype(BF16)


def _attention_b(qb, kb, vb, km, vm, zb, tq, tk, sub):
    bn, _, s = vb.shape[0], vb.shape[1], vb.shape[3]
    grid = (bn, N_KV, s // tq)
    qtspec = pl.BlockSpec((1, GROUP, HEAD_DIM, tq), lambda b, g, i: (b, g, 0, i))
    qspec = pl.BlockSpec((1, tq, GROUP * HEAD_DIM), lambda b, g, i: (b, i, g))
    kspec = pl.BlockSpec((1, 1, s, HEAD_DIM), lambda b, g, i: (b, g, 0, 0))
    vspec = pl.BlockSpec((1, 1, VT_ROWS, s), lambda b, g, i: (b, g, 0, 0))
    kmeta = pl.BlockSpec((1, 1, N_META, HEAD_DIM), lambda b, g, i: (0, g, 0, 0))
    vmeta = pl.BlockSpec((1, 1, VT_ROWS, N_META), lambda b, g, i: (0, g, 0, 0))
    return pl.pallas_call(
        functools.partial(_attn_b_kernel, tq=tq, tk=tk, sub=sub),
        grid=grid,
        in_specs=[qtspec, kspec, vspec, kmeta, vmeta, qspec],
        out_specs=qspec,
        out_shape=jax.ShapeDtypeStruct((bn, s, WIDTH), BF16),
        scratch_shapes=[pltpu.VMEM((tk, GROUP * tq), F32), pltpu.VMEM((tk, GROUP * tq), F32)],
        compiler_params=pltpu.CompilerParams(
            dimension_semantics=("arbitrary", "arbitrary", "arbitrary"),
            vmem_limit_bytes=VMEM_LIMIT),
        name="attn_b",
    )(qb, kb, vb, km, vm, zb)


def _out_kernel(ya_ref, yb_ref, g_ref, x_ref, wa_ref, wb_ref, wo_ref, ln_ref, o_ref):
    ua = jnp.dot(ya_ref[0], wa_ref[...], preferred_element_type=F32)
    ub = jnp.dot(yb_ref[0], wb_ref[...], preferred_element_type=F32)
    merged = (jax.nn.sigmoid(g_ref[0, :, :D_MODEL]) * ua
              + jax.nn.sigmoid(g_ref[0, :, D_MODEL:]) * ub)
    out = jnp.dot(merged.astype(BF16), wo_ref[...], preferred_element_type=F32)
    y = ALPHA * x_ref[0] + out
    mu = jnp.mean(y, axis=1, keepdims=True)
    d = y - mu
    var = jnp.mean(d * d, axis=1, keepdims=True)
    o_ref[0] = d * lax.rsqrt(var + LN_EPS) * ln_ref[0:1, :] + ln_ref[1:2, :]


def _output(ya, yb, gates, x, wa_bf, wb_bf, wo_bf, ln, tm):
    bn, s, _ = x.shape
    grid = (bn, s // tm)
    tok = lambda n: pl.BlockSpec((1, tm, n), lambda b, i: (b, i, 0))
    const = lambda shape: pl.BlockSpec(shape, lambda b, i: (0,) * len(shape),
                                       pipeline_mode=pl.Buffered(1))
    return pl.pallas_call(
        _out_kernel,
        grid=grid,
        in_specs=[tok(WIDTH), tok(WIDTH), tok(2 * D_MODEL), tok(D_MODEL),
                  const((WIDTH, D_MODEL)), const((WIDTH, D_MODEL)), const((D_MODEL, D_MODEL)),
                  const((2, D_MODEL))],
        out_specs=tok(D_MODEL),
        out_shape=jax.ShapeDtypeStruct((bn, s, D_MODEL), F32),
        compiler_params=pltpu.CompilerParams(
            dimension_semantics=("arbitrary", "arbitrary"), vmem_limit_bytes=VMEM_LIMIT),
        name="out",
    )(ya, yb, gates, x, wa_bf, wb_bf, wo_bf, ln)


def _encode(x, meta_kv, w_in_bf, tables, gains, sink_rows, wa_bf, wb_bf, wo_bf, ln):
    kma, vma, kmb, vmb = meta_kv
    qa, ka, va, za, qb, kb, vb, zb, gates = _project(x, w_in_bf, tables, gains, tm=512)
    ya = _attention_a(qa, ka, va, kma, vma, sink_rows, za, nblk=4)
    yb = _attention_b(qb, kb, vb, kmb, vmb, zb, tq=128, tk=2048, sub=256)
    return _output(ya, yb, gates, x, wa_bf, wb_bf, wo_bf, ln, tm=512)


def kernel(x_prompt, x_sample, meta_tokens, w_in, attn_a_sink, q_norm_b, k_norm_b,
           w_branch_a, w_branch_b, w_out, ln_gain, ln_bias):
    assert w_in.shape[0] == DEPTH
    s = x_prompt.shape[1]
    assert x_sample.shape[1] == s and s % GRID_W == 0
    w_in_bf = w_in[0].astype(BF16)
    wa_bf = w_branch_a[0].astype(BF16)
    wb_bf = w_branch_b[0].astype(BF16)
    wo_bf = w_out[0].astype(BF16)
    gains = jnp.stack([jnp.tile(q_norm_b[0], 2), jnp.tile(k_norm_b[0], 2)]).astype(F32)
    ln = jnp.stack([ln_gain[0], ln_bias[0]]).astype(F32)
    sink_rows = jnp.repeat(attn_a_sink[0].astype(F32).reshape(N_KV, GROUP) * LOG2E, BLOCK,
                           axis=1)[:, None, :]

    meta_pos = jnp.arange(N_META) - N_META
    rows = s // GRID_W
    tables = _rope_tables(jnp.arange(s) + N_META,
                          jnp.repeat(jnp.arange(rows), GRID_W),
                          jnp.tile(jnp.arange(GRID_W), rows))
    pad = BLOCK - N_META
    meta_pos_pad = jnp.pad(meta_pos, (0, pad))
    meta_tables = _rope_tables(jnp.arange(BLOCK), meta_pos_pad, meta_pos_pad)
    meta_x = jnp.pad(meta_tokens.astype(F32), ((0, pad), (0, 0)))[None]
    mp = _project(meta_x, w_in_bf, meta_tables, gains, tm=BLOCK)
    meta_kv = (mp[1][:, :, :N_META], mp[2][:, :, :, :N_META], mp[5][:, :, :N_META],
               mp[6][:, :, :, :N_META])

    enc = functools.partial(_encode, meta_kv=meta_kv, w_in_bf=w_in_bf, tables=tables, gains=gains,
                            sink_rows=sink_rows, wa_bf=wa_bf, wb_bf=wb_bf, wo_bf=wo_bf, ln=ln)
    return (enc(x_prompt), enc(x_sample))
```

```python
import functools

import numpy as np
import jax
import jax.numpy as jnp
from jax import lax
from jax.experimental import pallas as pl
from jax.experimental.pallas import tpu as pltpu

D_MODEL = 1024
HEAD_DIM = 64
N_HEADS = 8
N_KV = 2
GROUP = N_HEADS // N_KV
WIDTH = N_HEADS * HEAD_DIM
KV_WIDTH = N_KV * HEAD_DIM
WINDOW = 128
BLOCK = 128
N_META = 16
GRID_W = 64
ROPE_THETA = 10000.0
LN_EPS = 1e-5
RMS_EPS = 1e-6
NEG_INF = -1e30
DEPTH = 1
ALPHA = (2.0 * DEPTH) ** 0.25
SCALE = HEAD_DIM ** -0.5
LOG2E = 1.4426950408889634
MAX_ROWS = 8

LANES = 128
VT_ROWS = HEAD_DIM + 16
VMEM_LIMIT = 56 * 1024 * 1024

OFF_QA = 0
OFF_KA = OFF_QA + WIDTH
OFF_VA = OFF_KA + KV_WIDTH
OFF_ZA = OFF_VA + KV_WIDTH
OFF_QB = OFF_ZA + WIDTH
OFF_KB = OFF_QB + WIDTH
OFF_VB = OFF_KB + KV_WIDTH
OFF_ZB = OFF_VB + KV_WIDTH
OFF_G = OFF_ZB + WIDTH
D_IN = OFF_G + 2 * D_MODEL

BF16 = jnp.bfloat16
F32 = jnp.float32


def _rope_angles(pos, dim):
    inv = ROPE_THETA ** (-jnp.arange(0, dim, 2, dtype=F32) / dim)
    return pos.astype(F32)[:, None] * inv[None, :]


def _rope_tables(pos_a, row, col):
    lane = np.arange(LANES)
    ang_a = _rope_angles(pos_a, HEAD_DIM)
    cos_a = jnp.tile(jnp.cos(ang_a), (1, 4))
    sin_a = jnp.tile(jnp.sin(ang_a), (1, 4))
    lo_a = jnp.asarray((lane % 64) < 32)[None, :]
    h = HEAD_DIM // 2
    ang_r = _rope_angles(row, h)
    ang_c = _rope_angles(col, h)
    cos_b = jnp.tile(jnp.concatenate([jnp.cos(ang_r)] * 2 + [jnp.cos(ang_c)] * 2, axis=1), (1, 2))
    sin_b = jnp.tile(jnp.concatenate([jnp.sin(ang_r)] * 2 + [jnp.sin(ang_c)] * 2, axis=1), (1, 2))
    lo_b = jnp.asarray((lane % 32) < 16)[None, :]
    zero = jnp.zeros_like(sin_a)
    return jnp.concatenate([
        cos_a, jnp.where(lo_a, -sin_a, zero), jnp.where(lo_a, zero, sin_a),
        cos_b, jnp.where(lo_b, -sin_b, zero), jnp.where(lo_b, zero, sin_b)], axis=1)


def _rope(x, cos, sin_lo, sin_hi, half):
    return (x * cos + pltpu.roll(x, LANES - half, 1) * sin_lo
            + pltpu.roll(x, half, 1) * sin_hi)


def _head_rms(x, gain):
    xx = x * x
    s0 = jnp.sum(xx[:, :HEAD_DIM], axis=1, keepdims=True)
    s1 = jnp.sum(xx[:, HEAD_DIM:], axis=1, keepdims=True)
    lane = lax.broadcasted_iota(jnp.int32, x.shape, 1)
    ms = jnp.where(lane < HEAD_DIM, s0, s1) * (1.0 / HEAD_DIM)
    return x * lax.rsqrt(ms + RMS_EPS) * gain


def _proj_kernel(x_ref, w_ref, tab_ref, gain_ref,
                 qa_ref, ka_ref, va_ref, za_ref, qb_ref, kb_ref, vb_ref, zb_ref, g_ref):
    xb = x_ref[0].astype(BF16)

    def mm(off, n):
        return jnp.dot(xb, w_ref[:, off:off + n], preferred_element_type=F32)

    def tab(i):
        return tab_ref[:, i * LANES:(i + 1) * LANES]

    cos_a, slo_a, shi_a, cos_b, slo_b, shi_b = (tab(i) for i in range(6))
    gq = gain_ref[0:1, :]
    gk = gain_ref[1:2, :]

    tm = xb.shape[0]
    row = lax.broadcasted_iota(jnp.int32, (VT_ROWS - HEAD_DIM, tm), 0)
    ones_row = jnp.where(row == 0, 1.0, 0.0).astype(BF16)

    def put_q(ref, c, x):
        xt = (x * (SCALE * LOG2E)).T.astype(BF16)
        ref[0, 2 * c] = xt[:HEAD_DIM]
        ref[0, 2 * c + 1] = xt[HEAD_DIM:]

    def put_k(ref, x):
        ref[0, 0] = x[:, :HEAD_DIM].astype(BF16)
        ref[0, 1] = x[:, HEAD_DIM:].astype(BF16)

    def put_v(ref, x):
        xt = x.T.astype(BF16)
        for h in range(N_KV):
            ref[0, h, :HEAD_DIM, :] = xt[h * HEAD_DIM:(h + 1) * HEAD_DIM]
            ref[0, h, HEAD_DIM:, :] = ones_row

    qa = mm(OFF_QA, WIDTH)
    for c in range(WIDTH // LANES):
        put_q(qa_ref, c, _rope(qa[:, c * LANES:(c + 1) * LANES], cos_a, slo_a, shi_a, HEAD_DIM // 2))
    kva = mm(OFF_KA, 2 * KV_WIDTH)
    put_k(ka_ref, _rope(kva[:, :KV_WIDTH], cos_a, slo_a, shi_a, HEAD_DIM // 2))
    put_v(va_ref, kva[:, KV_WIDTH:])
    za_ref[0] = mm(OFF_ZA, WIDTH)

    qb = mm(OFF_QB, WIDTH)
    for c in range(WIDTH // LANES):
        xc = _head_rms(qb[:, c * LANES:(c + 1) * LANES], gq)
        put_q(qb_ref, c, _rope(xc, cos_b, slo_b, shi_b, HEAD_DIM // 4))
    kvb = mm(OFF_KB, 2 * KV_WIDTH)
    put_k(kb_ref, _rope(_head_rms(kvb[:, :KV_WIDTH], gk), cos_b, slo_b, shi_b, HEAD_DIM // 4))
    put_v(vb_ref, kvb[:, KV_WIDTH:])
    zb_ref[0] = mm(OFF_ZB, WIDTH)

    for c in range(2 * D_MODEL // 512):
        g_ref[0, :, c * 512:(c + 1) * 512] = mm(OFF_G + c * 512, 512)


def _project(x, w_in_bf, tables, gains, tm):
    bn, s, _ = x.shape
    grid = (s // tm, bn)
    tok = lambda n: pl.BlockSpec((1, tm, n), lambda i, b: (b, i, 0))
    kv = pl.BlockSpec((1, N_KV, tm, HEAD_DIM), lambda i, b: (b, 0, i, 0))
    kvt = pl.BlockSpec((1, N_KV, VT_ROWS, tm), lambda i, b: (b, 0, 0, i))
    qt = pl.BlockSpec((1, N_HEADS, HEAD_DIM, tm), lambda i, b: (b, 0, 0, i))
    const = lambda shape: pl.BlockSpec(shape, lambda i, b: (0,) * len(shape),
                                       pipeline_mode=pl.Buffered(1))
    q_shape = jax.ShapeDtypeStruct((bn, N_HEADS, HEAD_DIM, s), BF16)
    k_shape = jax.ShapeDtypeStruct((bn, N_KV, s, HEAD_DIM), BF16)
    v_shape = jax.ShapeDtypeStruct((bn, N_KV, VT_ROWS, s), BF16)
    z_shape = jax.ShapeDtypeStruct((bn, s, WIDTH), F32)
    out_shape = (q_shape, k_shape, v_shape, z_shape,
                 q_shape, k_shape, v_shape, z_shape,
                 jax.ShapeDtypeStruct((bn, s, 2 * D_MODEL), F32))
    return pl.pallas_call(
        _proj_kernel,
        grid=grid,
        in_specs=[tok(D_MODEL), const((D_MODEL, D_IN)),
                  pl.BlockSpec((tm, 6 * LANES), lambda i, b: (i, 0)),
                  const((2, LANES))],
        out_specs=(qt, kv, kvt, tok(WIDTH), qt, kv, kvt, tok(WIDTH), tok(2 * D_MODEL)),
        out_shape=out_shape,
        compiler_params=pltpu.CompilerParams(
            dimension_semantics=("arbitrary", "arbitrary"), vmem_limit_bytes=VMEM_LIMIT),
        name="proj",
    )(x, w_in_bf, tables, gains)


def _stack_heads(q):
    return jnp.concatenate([q[:, h * HEAD_DIM:(h + 1) * HEAD_DIM] for h in range(GROUP)], axis=0)


def _unstack_heads(o, tq):
    return jnp.concatenate([o[h * tq:(h + 1) * tq] for h in range(GROUP)], axis=1)


def _qk(q, k):
    return lax.dot_general(q, k, (((1,), (1,)), ((), ())), preferred_element_type=F32)


def _silu(z):
    return z * jax.nn.sigmoid(z)


def _attn_a_kernel(q_ref, kp_ref, kc_ref, kn_ref, vp_ref, vc_ref, vn_ref, km_ref, vm_ref,
                   sink_ref, z_ref, o_ref, *, nblk):
    assert nblk >= 2
    i = pl.program_id(2)
    last = pl.num_programs(2) - 1
    kwin =jnp.concatenate([kp_ref[0, 0], kc_ref[0, 0], kn_ref[0, 0]], axis=0)
    vwin = jnp.concatenate([vp_ref[0, 0], vc_ref[0, 0], vn_ref[0, 0]], axis=1)
    r = lax.broadcasted_iota(jnp.int32, (3 * BLOCK, BLOCK), 0)
    c = lax.broadcasted_iota(jnp.int32, (3 * BLOCK, BLOCK), 1)
    band = jnp.abs(r - BLOCK - c) <= WINDOW
    sink = sink_ref[0]
    for j in range(nblk):
        q4t = jnp.concatenate([q_ref[0, h, :, j * BLOCK:(j + 1) * BLOCK] for h in range(GROUP)],
                              axis=1)
        valid = band
        if j == 0:
            valid = valid & ((r >= BLOCK) | (i > 0))
        if j == nblk - 1:
            valid = valid & ((r < 2 * BLOCK) | (i < last))
        s = jnp.dot(kwin[j * BLOCK:(j + 3) * BLOCK], q4t, preferred_element_type=F32)
        s = jnp.concatenate(
            [jnp.where(valid, s[:, h * BLOCK:(h + 1) * BLOCK], NEG_INF) for h in range(GROUP)], axis=1)
        sm = jnp.dot(km_ref[0, 0], q4t, preferred_element_type=F32)
        m = jnp.maximum(jnp.maximum(jnp.max(s, axis=0, keepdims=True),
                                    jnp.max(sm, axis=0, keepdims=True)), sink)
        p = jnp.exp2(s - m).astype(BF16)
        pm = jnp.exp2(sm - m).astype(BF16)
        o_t = (jnp.dot(vwin[:, j * BLOCK:(j + 3) * BLOCK], p, preferred_element_type=F32)
               + jnp.dot(vm_ref[0, 0], pm, preferred_element_type=F32))
        l = o_t[HEAD_DIM:HEAD_DIM + 1] + jnp.exp2(sink - m)
        o_t = o_t[:HEAD_DIM] / l
        o = jnp.concatenate([o_t[:, h * BLOCK:(h + 1) * BLOCK] for h in range(GROUP)], axis=0).T
        rows = slice(j * BLOCK, (j + 1) * BLOCK)
        o_ref[0, rows, :] = (o * _silu(z_ref[0, rows, :])).astype(BF16)


def _attention_a(qa, ka, va, km, vm, sink_rows, za, nblk):
    bn, s = va.shape[0], va.shape[3]
    nb = s // BLOCK
    grid = (bn, N_KV, nb // nblk)
    tq = nblk * BLOCK
    qtspec = pl.BlockSpec((1, GROUP, HEAD_DIM, tq), lambda b, g, i: (b, g, 0, i))
    zspec = pl.BlockSpec((1, tq, GROUP * HEAD_DIM), lambda b, g, i: (b, i, g))
    prev = lambda i: jnp.maximum(i * nblk - 1, 0)
    nxt = lambda i: jnp.minimum((i + 1) * nblk, nb - 1)
    kprev = pl.BlockSpec((1, 1, BLOCK, HEAD_DIM), lambda b, g, i: (b, g, prev(i), 0))
    kcur = pl.BlockSpec((1, 1, tq, HEAD_DIM), lambda b, g, i: (b, g, i, 0))
    knext = pl.BlockSpec((1, 1, BLOCK, HEAD_DIM), lambda b, g, i: (b, g, nxt(i), 0))
    vprev = pl.BlockSpec((1, 1, VT_ROWS, BLOCK), lambda b, g, i: (b, g, 0, prev(i)))
    vcur = pl.BlockSpec((1, 1, VT_ROWS, tq), lambda b, g, i: (b, g, 0, i))
    vnext = pl.BlockSpec((1, 1, VT_ROWS, BLOCK), lambda b, g, i: (b, g, 0, nxt(i)))
    kmeta = pl.BlockSpec((1, 1, N_META, HEAD_DIM), lambda b, g, i: (0, g, 0, 0))
    vmeta = pl.BlockSpec((1, 1, VT_ROWS, N_META), lambda b, g, i: (0, g, 0, 0))
    sink = pl.BlockSpec((1, 1, GROUP * BLOCK), lambda b, g, i: (g, 0, 0))
    return pl.pallas_call(
        functools.partial(_attn_a_kernel, nblk=nblk),
        grid=grid,
        in_specs=[qtspec, kprev, kcur, knext, vprev, vcur, vnext, kmeta, vmeta, sink, zspec],
        out_specs=zspec,
        out_shape=jax.ShapeDtypeStruct((bn, s, WIDTH), BF16),
        compiler_params=pltpu.CompilerParams(
            dimension_semantics=("arbitrary", "arbitrary", "arbitrary"),
            vmem_limit_bytes=VMEM_LIMIT),
        name="attn_a",
    )(qa, ka, ka, ka, va, va, va, km, vm, sink_rows, za)


def _attn_b_kernel(q_ref, qn_ref, k_ref, v_ref, km_ref, vm_ref, z_ref, o_ref,
                   s0_ref, s1_ref, part_ref, sm_ref, *, tq, tk, sub):
    mq = GROUP * tq
    n_chunks = k_ref.shape[2] // tk
    assert n_chunks % 2 == 0
    s_refs = (s0_ref, s1_ref)

    def heads_on_lanes(ref):
        return jnp.concatenate([ref[0, h] for h in range(GROUP)], axis=1)

    def key_block(c, u):
        start = c * tk + u * sub
        return pl.ds(start if isinstance(start, int) else pl.multiple_of(start, sub), sub)

    q4t = heads_on_lanes(q_ref)

    def step(c, slot, m, part, acc, c_next, q_next, meta_now=False, meta_next=False):
        m_new = jnp.maximum(m, jnp.max(part, axis=0, keepdims=True))
        part_next, pv = None, None
        if meta_now:
            p_t = jnp.exp2(sm_ref[...] - m_new).astype(BF16)
            pv = jnp.dot(vm_ref[0, 0], p_t, preferred_element_type=F32)
        for u in range(tk // sub):
            rows = slice(u * sub, (u + 1) * sub)
            blk = jnp.dot(k_ref[0, 0, key_block(c_next, u), :], q_next,
                          preferred_element_type=F32)
            s_refs[1 - slot][rows, :] = blk
            bm = jnp.max(blk.reshape(sub // MAX_ROWS, MAX_ROWS, mq), axis=0)
            part_next = bm if part_next is None else jnp.maximum(part_next, bm)
            p_t = jnp.exp2(s_refs[slot][rows, :] - m_new).astype(BF16)
            d = jnp.dot(v_ref[0, 0, :, key_block(c, u)], p_t, preferred_element_type=F32)
            pv = d if pv is None else pv + d
        if meta_next:
            sm_ref[...] = jnp.dot(km_ref[0, 0], q_next, preferred_element_type=F32)
        return m_new, part_next, jnp.exp2(m - m_new) * acc + pv

    @pl.when(pl.program_id(2) == 0)
    def _():
        s0 = jnp.dot(k_ref[0, 0, 0:tk, :], q4t, preferred_element_type=F32)
        s0_ref[...] = s0
        part_ref[...] = jnp.max(s0.reshape(tk // MAX_ROWS, MAX_ROWS, mq), axis=0)
        sm_ref[...] = jnp.dot(km_ref[0, 0], q4t, preferred_element_type=F32)

    m = jnp.max(sm_ref[...], axis=0, keepdims=True)
    acc = jnp.zeros((VT_ROWS, mq), F32)
    m, part, acc = step(0, 0, m, part_ref[...], acc, 1, q4t, meta_now=True)

    def body(c, carry):
        return lax.cond(c % 2 == 0,
                        lambda cr: step(c, 0, *cr, c + 1, q4t),
                        lambda cr: step(c, 1, *cr, c + 1, q4t), carry)

    m, part, acc = lax.fori_loop(1, n_chunks - 1, body, (m, part, acc))
    _, part, acc = step(n_chunks - 1, 1, m, part, acc, 0, heads_on_lanes(qn_ref), meta_next=True)
    part_ref[...] = part
    o_t = acc[:HEAD_DIM] / acc[HEAD_DIM:HEAD_DIM + 1]
    o = jnp.concatenate([o_t[:, h * tq:(h + 1) * tq] for h in range(GROUP)], axis=0).T
    o_ref[0] = (o * _silu(z_ref[0])).astype(BF16)


def _attention_b(qb, kb, vb, km, vm, zb, tq, tk, sub):
    bn, _, s = vb.shape[0], vb.shape[1], vb.shape[3]
    grid = (bn, N_KV, s // tq)
    last = s // tq - 1
    qtspec = pl.BlockSpec((1, GROUP, HEAD_DIM, tq), lambda b, g, i: (b, g, 0, i))
    qnspec = pl.BlockSpec((1, GROUP, HEAD_DIM, tq), lambda b, g, i: (b, g, 0, jnp.minimum(i + 1, last)))
    qspec = pl.BlockSpec((1, tq, GROUP * HEAD_DIM), lambda b, g, i: (b, i, g))
    kspec = pl.BlockSpec((1, 1, s, HEAD_DIM), lambda b, g, i: (b, g, 0, 0))
    vspec = pl.BlockSpec((1, 1, VT_ROWS, s), lambda b, g, i: (b, g, 0, 0))
    kmeta = pl.BlockSpec((1, 1, N_META, HEAD_DIM), lambda b, g, i: (0, g, 0, 0))
    vmeta = pl.BlockSpec((1, 1, VT_ROWS, N_META), lambda b, g, i: (0, g, 0, 0))
    return pl.pallas_call(
        functools.partial(_attn_b_kernel, tq=tq, tk=tk, sub=sub),
        grid=grid,
        in_specs=[qtspec, qnspec, kspec, vspec, kmeta, vmeta, qspec],
        out_specs=qspec,
        out_shape=jax.ShapeDtypeStruct((bn, s, WIDTH), BF16),
        scratch_shapes=[pltpu.VMEM((tk, GROUP * tq), F32), pltpu.VMEM((tk, GROUP * tq), F32),
                        pltpu.VMEM((MAX_ROWS, GROUP * tq), F32),
                        pltpu.VMEM((N_META, GROUP * tq), F32)],
        compiler_params=pltpu.CompilerParams(
            dimension_semantics=("arbitrary", "arbitrary", "arbitrary"),
            vmem_limit_bytes=VMEM_LIMIT),
        name="attn_b",
    )(qb, qb, kb, vb, km, vm, zb)


def _out_kernel(ya_ref, yb_ref, g_ref, x_ref, wa_ref, wb_ref, wo_ref, ln_ref, o_ref):
    ua = jnp.dot(ya_ref[0], wa_ref[...], preferred_element_type=F32)
    ub = jnp.dot(yb_ref[0], wb_ref[...], preferred_element_type=F32)
    merged = (jax.nn.sigmoid(g_ref[0, :, :D_MODEL]) * ua
              + jax.nn.sigmoid(g_ref[0, :, D_MODEL:]) * ub)
    out = jnp.dot(merged.astype(BF16), wo_ref[...], preferred_element_type=F32)
    y = ALPHA * x_ref[0] + out
    mu = jnp.mean(y, axis=1, keepdims=True)
    d = y - mu
    var = jnp.mean(d * d, axis=1, keepdims=True)
    o_ref[0] = d * lax.rsqrt(var + LN_EPS) * ln_ref[0:1, :] + ln_ref[1:2, :]


def _output(ya, yb, gates, x, wa_bf, wb_bf, wo_bf, ln, tm):
    bn, s, _ = x.shape
    grid = (bn, s // tm)
    tok = lambda n: pl.BlockSpec((1, tm, n), lambda b, i: (b, i, 0))
    const = lambda shape: pl.BlockSpec(shape, lambda b, i: (0,) * len(shape),
                                       pipeline_mode=pl.Buffered(1))
    return pl.pallas_call(
        _out_kernel,
        grid=grid,
        in_specs=[tok(WIDTH), tok(WIDTH), tok(2 * D_MODEL), tok(D_MODEL),
                  const((WIDTH, D_MODEL)), const((WIDTH, D_MODEL)), const((D_MODEL, D_MODEL)),
                  const((2, D_MODEL))],
        out_specs=tok(D_MODEL),
        out_shape=jax.ShapeDtypeStruct((bn, s, D_MODEL), F32),
        compiler_params=pltpu.CompilerParams(
            dimension_semantics=("arbitrary", "arbitrary"), vmem_limit_bytes=VMEM_LIMIT),
        name="out",
    )(ya, yb, gates, x, wa_bf, wb_bf, wo_bf, ln)


def _encode(x, meta_kv, w_in_bf, tables, gains, sink_rows, wa_bf, wb_bf, wo_bf, ln):
    kma, vma, kmb, vmb = meta_kv
    qa, ka, va, za, qb, kb, vb, zb, gates = _project(x, w_in_bf, tables, gains, tm=512)
    ya = _attention_a(qa, ka, va, kma, vma, sink_rows, za, nblk=8)
    yb = _attention_b(qb, kb, vb, kmb, vmb, zb, tq=128, tk=2048, sub=256)
    return _output(ya, yb, gates, x, wa_bf, wb_bf, wo_bf, ln, tm=512)


def kernel(x_prompt, x_sample, meta_tokens, w_in, attn_a_sink, q_norm_b, k_norm_b,
           w_branch_a, w_branch_b, w_out, ln_gain, ln_bias):
    assert w_in.shape[0] == DEPTH
    s = x_prompt.shape[1]
    assert x_sample.shape[1] == s and s % GRID_W == 0
    w_in_bf = w_in[0].astype(BF16)
    wa_bf = w_branch_a[0].astype(BF16)
    wb_bf = w_branch_b[0].astype(BF16)
    wo_bf = w_out[0].astype(BF16)
    gains = jnp.stack([jnp.tile(q_norm_b[0], 2), jnp.tile(k_norm_b[0], 2)]).astype(F32)
    ln = jnp.stack([ln_gain[0], ln_bias[0]]).astype(F32)
    sink_rows = jnp.repeat(attn_a_sink[0].astype(F32).reshape(N_KV, GROUP) * LOG2E, BLOCK,
                           axis=1)[:, None, :]

    meta_pos = jnp.arange(N_META) - N_META
    rows = s // GRID_W
    tables = _rope_tables(jnp.arange(s) + N_META,
                          jnp.repeat(jnp.arange(rows), GRID_W),
                          jnp.tile(jnp.arange(GRID_W), rows))
    pad = BLOCK - N_META
    meta_pos_pad = jnp.pad(meta_pos, (0, pad))
    meta_tables = _rope_tables(jnp.arange(BLOCK), meta_pos_pad, meta_pos_pad)
    meta_x = jnp.pad(meta_tokens.astype(F32), ((0, pad), (0, 0)))[None]
    mp = _project(meta_x, w_in_bf, meta_tables, gains, tm=BLOCK)
    meta_kv = (mp[1][:, :, :N_META], mp[2][:, :, :, :N_META], mp[5][:, :, :N_META],
               mp[6][:, :, :, :N_META])

    enc = functools.partial(_encode, meta_kv=meta_kv, w_in_bf=w_in_bf, tables=tables, gains=gains,
                            sink_rows=sink_rows, wa_bf=wa_bf, wb_bf=wb_bf, wo_bf=wo_bf, ln=ln)
    return (enc(x_prompt), enc(x_sample))
```

```python
import functools

import numpy as np
import jax
import jax.numpy as jnp
from jax import lax
from jax.experimental import pallas as pl
from jax.experimental.pallas import tpu as pltpu

D_MODEL = 1024
HEAD_DIM = 64
N_HEADS = 8
N_KV = 2
GROUP = N_HEADS // N_KV
WIDTH = N_HEADS * HEAD_DIM
KV_WIDTH = N_KV * HEAD_DIM
WINDOW = 128
BLOCK = 128
N_META = 16
GRID_W = 64
ROPE_THETA = 10000.0
LN_EPS = 1e-5
RMS_EPS = 1e-6
NEG_INF = -1e30
DEPTH = 1
ALPHA = (2.0 * DEPTH) ** 0.25
SCALE = HEAD_DIM ** -0.5
LOG2E = 1.4426950408889634
MAX_ROWS = 8

LANES = 128
VT_ROWS = HEAD_DIM + 16
VMEM_LIMIT = 56 * 1024 * 1024

OFF_QA = 0
OFF_KA = OFF_QA + WIDTH
OFF_VA = OFF_KA + KV_WIDTH
OFF_ZA = OFF_VA + KV_WIDTH
OFF_QB = OFF_ZA + WIDTH
OFF_KB = OFF_QB + WIDTH
OFF_VB = OFF_KB + KV_WIDTH
OFF_ZB = OFF_VB + KV_WIDTH
OFF_G = OFF_ZB + WIDTH
D_IN = OFF_G + 2 * D_MODEL

T_QA = 0
T_VA = T_QA + WIDTH
T_QB = T_VA + KV_WIDTH
T_VB = T_QB + WIDTH
T_ROWS = T_VB + KV_WIDTH
S_KA = 0
S_KB = S_KA + KV_WIDTH
S_ZA = S_KB + KV_WIDTH
S_ZB = S_ZA + WIDTH
S_G = S_ZB + WIDTH
S_COLS = S_G + 2 * D_MODEL

BF16 = jnp.bfloat16
F32 = jnp.float32


def _rope_angles(pos, dim):
    inv = ROPE_THETA ** (-jnp.arange(0, dim, 2, dtype=F32) / dim)
    return pos.astype(F32)[:, None] * inv[None, :]


def _rope_tables(pos_a, row, col):
    lane = np.arange(LANES)
    ang_a = _rope_angles(pos_a, HEAD_DIM)
    cos_a = jnp.tile(jnp.cos(ang_a), (1, 4))
    sin_a = jnp.tile(jnp.sin(ang_a), (1, 4))
    lo_a = jnp.asarray((lane % 64) < 32)[None, :]
    h = HEAD_DIM // 2
    ang_r = _rope_angles(row, h)
    ang_c = _rope_angles(col, h)
    cos_b = jnp.tile(jnp.concatenate([jnp.cos(ang_r)] * 2 + [jnp.cos(ang_c)] * 2, axis=1), (1, 2))
    sin_b = jnp.tile(jnp.concatenate([jnp.sin(ang_r)] * 2 + [jnp.sin(ang_c)] * 2, axis=1), (1, 2))
    lo_b = jnp.asarray((lane % 32) < 16)[None, :]
    zero = jnp.zeros_like(sin_a)
    token_major = jnp.concatenate([
        cos_a, jnp.where(lo_a, -sin_a, zero), jnp.where(lo_a, zero, sin_a),
        cos_b, jnp.where(lo_b, -sin_b, zero), jnp.where(lo_b, zero, sin_b)], axis=1)
    feature_major = jnp.concatenate([
        jnp.cos(ang_a), jnp.sin(ang_a), jnp.cos(ang_r), jnp.sin(ang_r),
        jnp.cos(ang_c), jnp.sin(ang_c)], axis=1).T
    return token_major, feature_major


def _rope(x, cos, sin_lo, sin_hi, half):
    return (x * cos + pltpu.roll(x, LANES - half, 1) * sin_lo
            + pltpu.roll(x, half, 1) * sin_hi)


def _head_rms(x, gain):
    xx = x * x
    s0 = jnp.sum(xx[:, :HEAD_DIM], axis=1, keepdims=True)
    s1 = jnp.sum(xx[:, HEAD_DIM:], axis=1, keepdims=True)
    lane = lax.broadcasted_iota(jnp.int32, x.shape, 1)
    ms = jnp.where(lane < HEAD_DIM, s0, s1) * (1.0 / HEAD_DIM)
    return x * lax.rsqrt(ms + RMS_EPS) * gain


def _rotate_rows(x, cos, sin):
    n = cos.shape[0]
    x1, x2 = x[:n], x[n:]
    return x1 * cos - x2 * sin, x1 * sin + x2 * cos


def _proj_kernel(x_ref, wt_ref, ws_ref, tab_ref, tabt_ref, gk_ref, gq_ref,
                 qa_ref, ka_ref, va_ref, za_ref, qb_ref, kb_ref, vb_ref, zb_ref, g_ref):
    xb = x_ref[0].astype(BF16)
    tm = xb.shape[0]
    qscale = SCALE * LOG2E

    t = lax.dot_general(wt_ref[...], xb, (((1,), (1,)), ((), ())), preferred_element_type=F32)

    row = lax.broadcasted_iota(jnp.int32, (VT_ROWS - HEAD_DIM, tm), 0)
    ones_row = jnp.where(row == 0, 1.0, 0.0).astype(BF16)

    def put_v(ref, vt):
        for h in range(N_KV):
            ref[0, h, :HEAD_DIM, :] = vt[h * HEAD_DIM:(h + 1) * HEAD_DIM].astype(BF16)
            ref[0, h, HEAD_DIM:, :] = ones_row

    half, quarter = HEAD_DIM // 2, HEAD_DIM // 4
    cos_a, sin_a = tabt_ref[0:half, :], tabt_ref[half:2 * half, :]
    for h in range(N_HEADS):
        lo, hi = _rotate_rows(t[h * HEAD_DIM:(h + 1) * HEAD_DIM], cos_a, sin_a)
        qa_ref[0, h, :half, :] = (lo * qscale).astype(BF16)
        qa_ref[0, h, half:, :] = (hi * qscale).astype(BF16)
    put_v(va_ref, t[T_VA:T_VA + KV_WIDTH])

    base = 2 * half
    cos_r, sin_r = tabt_ref[base:base + quarter, :], tabt_ref[base + quarter:base + 2 * quarter, :]
    base += 2 * quarter
    cos_c, sin_c = tabt_ref[base:base + quarter, :], tabt_ref[base + quarter:base + 2 * quarter, :]
    gq = jnp.concatenate([gq_ref[...]] * (tm // LANES), axis=1)
    for h in range(N_HEADS):
        xh = t[T_QB + h * HEAD_DIM:T_QB + (h + 1) * HEAD_DIM]
        ms = jnp.sum(xh * xh, axis=0, keepdims=True) * (1.0 / HEAD_DIM)
        y = xh * lax.rsqrt(ms + RMS_EPS) * gq
        parts = _rotate_rows(y[:half], cos_r, sin_r) + _rotate_rows(y[half:], cos_c, sin_c)
        for i, part in enumerate(parts):
            qb_ref[0, h, i * quarter:(i + 1) * quarter, :] = (part * qscale).astype(BF16)
    put_v(vb_ref, t[T_VB:T_VB + KV_WIDTH])

    def mm(off, n):
        return jnp.dot(xb, ws_ref[:, off:off + n], preferred_element_type=F32)

    def tab(i):
        return tab_ref[:, i * LANES:(i + 1) * LANES]

    def put_k(ref, x):
        ref[0, 0] = x[:, :HEAD_DIM].astype(BF16)
        ref[0, 1] = x[:, HEAD_DIM:].astype(BF16)

    k = mm(S_KA, 2 * KV_WIDTH)
    put_k(ka_ref, _rope(k[:, :KV_WIDTH], tab(0), tab(1), tab(2), half))
    put_k(kb_ref, _rope(_head_rms(k[:, KV_WIDTH:], gk_ref[...]), tab(3), tab(4), tab(5), quarter))
    za_ref[0] = mm(S_ZA, WIDTH)
    zb_ref[0] = mm(S_ZB, WIDTH)
    for c in range(2 * D_MODEL // 512):
        g_ref[0, :, c * 512:(c + 1) * 512] = mm(S_G + c * 512, 512)


def _split_w_in(w):
    cols = lambda off, n: w[:, off:off + n]
    w_t = jnp.concatenate([cols(OFF_QA, WIDTH), cols(OFF_VA, KV_WIDTH),
                           cols(OFF_QB, WIDTH), cols(OFF_VB, KV_WIDTH)], axis=1).T
    w_s = jnp.concatenate([cols(OFF_KA, KV_WIDTH), cols(OFF_KB, KV_WIDTH), cols(OFF_ZA, WIDTH),
                           cols(OFF_ZB, WIDTH), cols(OFF_G, 2 * D_MODEL)], axis=1)
    return w_t.astype(BF16), w_s.astype(BF16)


def _project(x, w_t, w_s, tables, gk, gq, tm):
    tab, tab_t = tables
    bn, s, _ = x.shape
    grid = (s // tm, bn)
    tok = lambda n: pl.BlockSpec((1, tm, n), lambda i, b: (b, i, 0))
    kv = pl.BlockSpec((1, N_KV, tm, HEAD_DIM), lambda i, b: (b, 0, i, 0))
    kvt = pl.BlockSpec((1, N_KV, VT_ROWS, tm), lambda i, b: (b, 0, 0, i))
    qt = pl.BlockSpec((1, N_HEADS, HEAD_DIM, tm), lambda i, b: (b, 0, 0, i))
    const = lambda shape: pl.BlockSpec(shape, lambda i, b: (0,) * len(shape),
                                       pipeline_mode=pl.Buffered(1))
    q_shape = jax.ShapeDtypeStruct((bn, N_HEADS, HEAD_DIM, s), BF16)
    k_shape = jax.ShapeDtypeStruct((bn, N_KV, s, HEAD_DIM), BF16)
    v_shape = jax.ShapeDtypeStruct((bn, N_KV, VT_ROWS, s), BF16)
    z_shape = jax.ShapeDtypeStruct((bn, s, WIDTH), F32)
    out_shape = (q_shape, k_shape, v_shape, z_shape,
                 q_shape, k_shape, v_shape, z_shape,
                 jax.ShapeDtypeStruct((bn, s, 2 * D_MODEL), F32))
    return pl.pallas_call(
        _proj_kernel,
        grid=grid,
        in_specs=[tok(D_MODEL), const((T_ROWS, D_MODEL)), const((D_MODEL, S_COLS)),
                  pl.BlockSpec((tm, 6 * LANES), lambda i, b: (i, 0)),
                  pl.BlockSpec((LANES, tm), lambda i, b: (0, i)),
                  const((1, LANES)), const((HEAD_DIM, LANES))],
        out_specs=(qt, kv, kvt, tok(WIDTH), qt, kv, kvt, tok(WIDTH), tok(2 * D_MODEL)),
        out_shape=out_shape,
        compiler_params=pltpu.CompilerParams(
            dimension_semantics=("arbitrary", "arbitrary"), vmem_limit_bytes=VMEM_LIMIT),
        name="proj",
    )(x, w_t, w_s, tab, tab_t, gk, gq)


def _stack_heads(q):
    return jnp.concatenate([q[:, h * HEAD_DIM:(h + 1) * HEAD_DIM] for h in range(GROUP)], axis=0)


def _unstack_heads(o, tq):
    return jnp.concatenate([o[h * tq:(h + 1) * tq] for h in range(GROUP)], axis=1)


def _qk(q, k):
    return lax.dot_general(q, k, (((1,), (1,)), ((), ())), preferred_element_type=F32)


def _silu(z):
    return z * jax.nn.sigmoid(z)


def _attn_a_kernel(q_ref, kp_ref, kc_ref, kn_ref, vp_ref, vc_ref, vn_ref, km_ref, vm_ref,
                   sink_ref, z_ref, o_ref, *, nblk):
    assert nblk >= 2
    i = pl.program_id(2)
    last = pl.num_programs(2) - 1
    kwin =jnp.concatenate([kp_ref[0, 0], kc_ref[0, 0], kn_ref[0, 0]], axis=0)
    vwin = jnp.concatenate([vp_ref[0, 0], vc_ref[0, 0], vn_ref[0, 0]], axis=1)
    r = lax.broadcasted_iota(jnp.int32, (3 * BLOCK, BLOCK), 0)
    c = lax.broadcasted_iota(jnp.int32, (3 * BLOCK, BLOCK), 1)
    band = jnp.abs(r - BLOCK - c) <= WINDOW
    sink = sink_ref[0]
    for j in range(nblk):
        q4t = jnp.concatenate([q_ref[0, h, :, j * BLOCK:(j + 1) * BLOCK] for h in range(GROUP)],
                              axis=1)
        valid = band
        if j == 0:
            valid = valid & ((r >= BLOCK) | (i > 0))
        if j == nblk - 1:
            valid = valid & ((r < 2 * BLOCK) | (i < last))
        s = jnp.dot(kwin[j * BLOCK:(j + 3) * BLOCK], q4t, preferred_element_type=F32)
        s = jnp.concatenate(
            [jnp.where(valid, s[:, h * BLOCK:(h + 1) * BLOCK], NEG_INF) for h in range(GROUP)], axis=1)
        sm = jnp.dot(km_ref[0, 0], q4t, preferred_element_type=F32)
        m = jnp.maximum(jnp.maximum(jnp.max(s, axis=0, keepdims=True),
                                    jnp.max(sm, axis=0, keepdims=True)), sink)
        p = jnp.exp2(s - m).astype(BF16)
        pm = jnp.exp2(sm - m).astype(BF16)
        o_t = (jnp.dot(vwin[:, j * BLOCK:(j + 3) * BLOCK], p, preferred_element_type=F32)
               + jnp.dot(vm_ref[0, 0], pm, preferred_element_type=F32))
        l = o_t[HEAD_DIM:HEAD_DIM + 1] + jnp.exp2(sink - m)
        o_t = o_t[:HEAD_DIM] / l
        o = jnp.concatenate([o_t[:, h * BLOCK:(h + 1) * BLOCK] for h in range(GROUP)], axis=0).T
        rows = slice(j * BLOCK, (j + 1) * BLOCK)
        o_ref[0, rows, :] = (o * _silu(z_ref[0, rows, :])).astype(BF16)


def _attention_a(qa, ka, va, km, vm, sink_rows, za, nblk):
    bn, s = va.shape[0], va.shape[3]
    nb = s // BLOCK
    grid = (bn, N_KV, nb // nblk)
    tq = nblk * BLOCK
    qtspec = pl.BlockSpec((1, GROUP, HEAD_DIM, tq), lambda b, g, i: (b, g, 0, i))
    zspec = pl.BlockSpec((1, tq, GROUP * HEAD_DIM), lambda b, g, i: (b, i, g))
    prev = lambda i: jnp.maximum(i * nblk - 1, 0)
    nxt = lambda i: jnp.minimum((i + 1) * nblk, nb - 1)
    kprev = pl.BlockSpec((1, 1, BLOCK, HEAD_DIM), lambda b, g, i: (b, g, prev(i), 0))
    kcur = pl.BlockSpec((1, 1, tq, HEAD_DIM), lambda b, g, i: (b, g, i, 0))
    knext = pl.BlockSpec((1, 1, BLOCK, HEAD_DIM), lambda b, g, i: (b, g, nxt(i), 0))
    vprev = pl.BlockSpec((1, 1, VT_ROWS, BLOCK), lambda b, g, i: (b, g, 0, prev(i)))
    vcur = pl.BlockSpec((1, 1, VT_ROWS, tq), lambda b, g, i: (b, g, 0, i))
    vnext = pl.BlockSpec((1, 1, VT_ROWS, BLOCK), lambda b, g, i: (b, g, 0, nxt(i)))
    kmeta = pl.BlockSpec((1, 1, N_META, HEAD_DIM), lambda b, g, i: (0, g, 0, 0))
    vmeta = pl.BlockSpec((1, 1, VT_ROWS, N_META), lambda b, g, i: (0, g, 0, 0))
    sink = pl.BlockSpec((1, 1, GROUP * BLOCK), lambda b, g, i: (g, 0, 0))
    return pl.pallas_call(
        functools.partial(_attn_a_kernel, nblk=nblk),
        grid=grid,
        in_specs=[qtspec, kprev, kcur, knext, vprev, vcur, vnext, kmeta, vmeta, sink, zspec],
        out_specs=zspec,
        out_shape=jax.ShapeDtypeStruct((bn, s, WIDTH), BF16),
        compiler_params=pltpu.CompilerParams(
            dimension_semantics=("arbitrary", "arbitrary", "arbitrary"),
            vmem_limit_bytes=VMEM_LIMIT),
        name="attn_a",
    )(qa, ka, ka, ka, va, va, va, km, vm, sink_rows, za)


def _attn_b_kernel(q_ref, qn_ref, k_ref, v_ref, km_ref, vm_ref, z_ref, o_ref,
                   s0_ref, s1_ref, part_ref, sm_ref, *, tq, tk, sub):
    mq = GROUP * tq
    n_chunks = k_ref.shape[2] // tk
    assert n_chunks % 2 == 0
    s_refs = (s0_ref, s1_ref)

    def heads_on_lanes(ref):
        return jnp.concatenate([ref[0, h] for h in range(GROUP)], axis=1)

    def key_block(c, u):
        start = c * tk + u * sub
        return pl.ds(start if isinstance(start, int) else pl.multiple_of(start, sub), sub)

    q4t = heads_on_lanes(q_ref)

    def step(c, slot, m, part, acc, c_next, q_next, meta_now=False, meta_next=False):
        m_new = jnp.maximum(m, jnp.max(part, axis=0, keepdims=True))
        part_next, pv = None, None
        if meta_now:
            p_t = jnp.exp2(sm_ref[...] - m_new).astype(BF16)
            pv = jnp.dot(vm_ref[0, 0], p_t, preferred_element_type=F32)
        for u in range(tk // sub):
            rows = slice(u * sub, (u + 1) * sub)
            blk = jnp.dot(k_ref[0, 0, key_block(c_next, u), :], q_next,
                          preferred_element_type=F32)
            s_refs[1 - slot][rows, :] = blk
            bm = jnp.max(blk.reshape(sub // MAX_ROWS, MAX_ROWS, mq), axis=0)
            part_next = bm if part_next is None else jnp.maximum(part_next, bm)
            p_t = jnp.exp2(s_refs[slot][rows, :] - m_new).astype(BF16)
            d = jnp.dot(v_ref[0, 0, :, key_block(c, u)], p_t, preferred_element_type=F32)
            pv = d if pv is None else pv + d
        if meta_next:
            sm_ref[...] = jnp.dot(km_ref[0, 0], q_next, preferred_element_type=F32)
        return m_new, part_next, jnp.exp2(m - m_new) * acc + pv

    @pl.when(pl.program_id(2) == 0)
    def _():
        s0 = jnp.dot(k_ref[0, 0, 0:tk, :], q4t, preferred_element_type=F32)
        s0_ref[...] = s0
        part_ref[...] = jnp.max(s0.reshape(tk // MAX_ROWS, MAX_ROWS, mq), axis=0)
        sm_ref[...] = jnp.dot(km_ref[0, 0], q4t, preferred_element_type=F32)

    m = jnp.max(sm_ref[...], axis=0, keepdims=True)
    acc = jnp.zeros((VT_ROWS, mq), F32)
    m, part, acc = step(0, 0, m, part_ref[...], acc, 1, q4t, meta_now=True)

    def body(c, carry):
        return lax.cond(c % 2 == 0,
                        lambda cr: step(c, 0, *cr, c + 1, q4t),
                        lambda cr: step(c, 1, *cr, c + 1, q4t), carry)

    m, part, acc = lax.fori_loop(1, n_chunks - 1, body, (m, part, acc))
    _, part, acc = step(n_chunks - 1, 1, m, part, acc, 0, heads_on_lanes(qn_ref), meta_next=True)
    part_ref[...] = part
    o_t = acc[:HEAD_DIM] / acc[HEAD_DIM:HEAD_DIM + 1]
    o = jnp.concatenate([o_t[:, h * tq:(h + 1) * tq] for h in range(GROUP)], axis=0).T
    o_ref[0] = (o * _silu(z_ref[0])).astype(BF16)


def _attention_b(qb, kb, vb, km, vm, zb, tq, tk, sub):
    bn, _, s = vb.shape[0], vb.shape[1], vb.shape[3]
    grid = (bn, N_KV, s // tq)
    last = s // tq - 1
    qtspec = pl.BlockSpec((1, GROUP, HEAD_DIM, tq), lambda b, g, i: (b, g, 0, i))
    qnspec = pl.BlockSpec((1, GROUP, HEAD_DIM, tq), lambda b, g, i: (b, g, 0, jnp.minimum(i + 1, last)))
    qspec = pl.BlockSpec((1, tq, GROUP * HEAD_DIM), lambda b, g, i: (b, i, g))
    kspec = pl.BlockSpec((1, 1, s, HEAD_DIM), lambda b, g, i: (b, g, 0, 0))
    vspec = pl.BlockSpec((1, 1, VT_ROWS, s), lambda b, g, i: (b, g, 0, 0))
    kmeta = pl.BlockSpec((1, 1, N_META, HEAD_DIM), lambda b, g, i: (0, g, 0, 0))
    vmeta = pl.BlockSpec((1, 1, VT_ROWS, N_META), lambda b, g, i: (0, g, 0, 0))
    return pl.pallas_call(
        functools.partial(_attn_b_kernel, tq=tq, tk=tk, sub=sub),
        grid=grid,
        in_specs=[qtspec, qnspec, kspec, vspec, kmeta, vmeta, qspec],
        out_specs=qspec,
        out_shape=jax.ShapeDtypeStruct((bn, s, WIDTH), BF16),
        scratch_shapes=[pltpu.VMEM((tk, GROUP * tq), F32), pltpu.VMEM((tk, GROUP * tq), F32),
                        pltpu.VMEM((MAX_ROWS, GROUP * tq), F32),
                        pltpu.VMEM((N_META, GROUP * tq), F32)],
        compiler_params=pltpu.CompilerParams(
            dimension_semantics=("arbitrary", "arbitrary", "arbitrary"),
            vmem_limit_bytes=VMEM_LIMIT),
        name="attn_b",
    )(qb, qb, kb, vb, km, vm, zb)


def _out_kernel(ya_ref, yb_ref, g_ref, x_ref, wa_ref, wb_ref, wo_ref, ln_ref, o_ref):
    ua = jnp.dot(ya_ref[0], wa_ref[...], preferred_element_type=F32)
    ub = jnp.dot(yb_ref[0], wb_ref[...], preferred_element_type=F32)
    merged = (jax.nn.sigmoid(g_ref[0, :, :D_MODEL]) * ua
              + jax.nn.sigmoid(g_ref[0, :, D_MODEL:]) * ub)
    out = jnp.dot(merged.astype(BF16), wo_ref[...], preferred_element_type=F32)
    y = ALPHA * x_ref[0] + out
    mu = jnp.mean(y, axis=1, keepdims=True)
    d = y - mu
    var = jnp.mean(d * d, axis=1, keepdims=True)
    o_ref[0] = d * lax.rsqrt(var + LN_EPS) * ln_ref[0:1, :] + ln_ref[1:2, :]


def _output(ya, yb, gates, x, wa_bf, wb_bf, wo_bf, ln, tm):
    bn, s, _ = x.shape
    grid = (bn, s // tm)
    tok = lambda n: pl.BlockSpec((1, tm, n), lambda b, i: (b, i, 0))
    const = lambda shape: pl.BlockSpec(shape, lambda b, i: (0,) * len(shape),
                                       pipeline_mode=pl.Buffered(1))
    return pl.pallas_call(
        _out_kernel,
        grid=grid,
        in_specs=[tok(WIDTH), tok(WIDTH), tok(2 * D_MODEL), tok(D_MODEL),
                  const((WIDTH, D_MODEL)), const((WIDTH, D_MODEL)), const((D_MODEL, D_MODEL)),
                  const((2, D_MODEL))],
        out_specs=tok(D_MODEL),
        out_shape=jax.ShapeDtypeStruct((bn, s, D_MODEL), F32),
        compiler_params=pltpu.CompilerParams(
            dimension_semantics=("arbitrary", "arbitrary"), vmem_limit_bytes=VMEM_LIMIT),
        name="out",
    )(ya, yb, gates, x, wa_bf, wb_bf, wo_bf, ln)


def _encode(x, meta_kv, proj_args, tables, sink_rows, wa_bf, wb_bf, wo_bf, ln):
    kma, vma, kmb, vmb = meta_kv
    w_t, w_s, gk, gq = proj_args
    qa, ka, va, za, qb, kb, vb, zb, gates = _project(x, w_t, w_s, tables, gk, gq, tm=512)
    ya = _attention_a(qa, ka, va, kma, vma, sink_rows, za, nblk=16)
    yb = _attention_b(qb, kb, vb, kmb, vmb, zb, tq=128, tk=2048, sub=256)
    return _output(ya, yb, gates, x, wa_bf, wb_bf, wo_bf, ln, tm=512)


def kernel(x_prompt, x_sample, meta_tokens, w_in, attn_a_sink, q_norm_b, k_norm_b,
           w_branch_a, w_branch_b, w_out, ln_gain, ln_bias):
    assert w_in.shape[0] == DEPTH
    s = x_prompt.shape[1]
    assert x_sample.shape[1] == s and s % GRID_W == 0
    w_t, w_s = _split_w_in(w_in[0])
    wa_bf = w_branch_a[0].astype(BF16)
    wb_bf = w_branch_b[0].astype(BF16)
    wo_bf = w_out[0].astype(BF16)
    gk = jnp.tile(k_norm_b[0].astype(F32), 2)[None, :]
    gq = jnp.broadcast_to(q_norm_b[0].astype(F32)[:, None], (HEAD_DIM, LANES))
    proj_args = (w_t, w_s, gk, gq)
    ln = jnp.stack([ln_gain[0], ln_bias[0]]).astype(F32)
    sink_rows = jnp.repeat(attn_a_sink[0].astype(F32).reshape(N_KV, GROUP) * LOG2E, BLOCK,
                           axis=1)[:, None, :]

    meta_pos = jnp.arange(N_META) - N_META
    rows = s // GRID_W
    tables = _rope_tables(jnp.arange(s) + N_META,
                          jnp.repeat(jnp.arange(rows), GRID_W),
                          jnp.tile(jnp.arange(GRID_W), rows))
    pad = BLOCK - N_META
    meta_pos_pad = jnp.pad(meta_pos, (0, pad))
    meta_tables = _rope_tables(jnp.arange(BLOCK), meta_pos_pad, meta_pos_pad)
    meta_x = jnp.pad(meta_tokens.astype(F32), ((0, pad), (0, 0)))[None]
    mp = _project(meta_x, w_t, w_s, meta_tables, gk, gq, tm=BLOCK)
    meta_kv = (mp[1][:, :, :N_META], mp[2][:, :, :, :N_META], mp[5][:, :, :N_META],
               mp[6][:, :, :, :N_META])

    enc = functools.partial(_encode, meta_kv=meta_kv, proj_args=proj_args, tables=tables,
                            sink_rows=sink_rows, wa_bf=wa_bf, wb_bf=wb_bf, wo_bf=wo_bf, ln=ln)
    return (enc(x_prompt), enc(x_sample))
```

```python
import functools

import numpy as np
import jax
import jax.numpy as jnp
from jax import lax
from jax.experimental import pallas as pl
from jax.experimental.pallas import tpu as pltpu

D_MODEL = 1024
HEAD_DIM = 64
N_HEADS = 8
N_KV = 2
GROUP = N_HEADS // N_KV
WIDTH = N_HEADS * HEAD_DIM
KV_WIDTH = N_KV * HEAD_DIM
WINDOW = 128
BLOCK = 128
N_META = 16
GRID_W = 64
ROPE_THETA = 10000.0
LN_EPS = 1e-5
RMS_EPS = 1e-6
NEG_INF = -1e30
DEPTH = 1
ALPHA = (2.0 * DEPTH) ** 0.25
SCALE = HEAD_DIM ** -0.5
LOG2E = 1.4426950408889634
MAX_ROWS = 8

LANES = 128
VT_ROWS = HEAD_DIM + 16
VMEM_LIMIT = 56 * 1024 * 1024

OFF_QA = 0
OFF_KA = OFF_QA + WIDTH
OFF_VA = OFF_KA + KV_WIDTH
OFF_ZA = OFF_VA + KV_WIDTH
OFF_QB = OFF_ZA + WIDTH
OFF_KB = OFF_QB + WIDTH
OFF_VB = OFF_KB + KV_WIDTH
OFF_ZB = OFF_VB + KV_WIDTH
OFF_G = OFF_ZB + WIDTH
D_IN = OFF_G + 2 * D_MODEL

T_QA = 0
T_VA = T_QA + WIDTH
T_QB = T_VA + KV_WIDTH
T_VB = T_QB + WIDTH
T_ROWS = T_VB + KV_WIDTH
S_KA = 0
S_KB = S_KA + KV_WIDTH
S_ZA = S_KB + KV_WIDTH
S_ZB = S_ZA + WIDTH
S_G = S_ZB + WIDTH
S_COLS = S_G + 2 * D_MODEL

BF16 = jnp.bfloat16
F32 = jnp.float32


def _rope_angles(pos, dim):
    inv = ROPE_THETA ** (-jnp.arange(0, dim, 2, dtype=F32) / dim)
    return pos.astype(F32)[:, None] * inv[None, :]


def _rope_tables(pos_a, row, col):
    lane = np.arange(LANES)
    ang_a = _rope_angles(pos_a, HEAD_DIM)
    cos_a = jnp.tile(jnp.cos(ang_a), (1, 4))
    sin_a = jnp.tile(jnp.sin(ang_a), (1, 4))
    lo_a = jnp.asarray((lane % 64) < 32)[None, :]
    h = HEAD_DIM // 2
    ang_r = _rope_angles(row, h)
    ang_c = _rope_angles(col, h)
    cos_b = jnp.tile(jnp.concatenate([jnp.cos(ang_r)] * 2 + [jnp.cos(ang_c)] * 2, axis=1), (1, 2))
    sin_b = jnp.tile(jnp.concatenate([jnp.sin(ang_r)] * 2 + [jnp.sin(ang_c)] * 2, axis=1), (1, 2))
    lo_b = jnp.asarray((lane % 32) < 16)[None, :]
    zero = jnp.zeros_like(sin_a)
    token_major = jnp.concatenate([
        cos_a, jnp.where(lo_a, -sin_a, zero), jnp.where(lo_a, zero, sin_a),
        cos_b, jnp.where(lo_b, -sin_b, zero), jnp.where(lo_b, zero, sin_b)], axis=1)
    feature_major = jnp.concatenate([
        jnp.cos(ang_a), jnp.sin(ang_a), jnp.cos(ang_r), jnp.sin(ang_r),
        jnp.cos(ang_c), jnp.sin(ang_c)], axis=1).T
    return token_major, feature_major


def _rope(x, cos, sin_lo, sin_hi, half):
    return (x * cos + pltpu.roll(x, LANES - half, 1) * sin_lo
            + pltpu.roll(x, half, 1) * sin_hi)


def _head_rms(x, gain):
    xx = x * x
    s0 = jnp.sum(xx[:, :HEAD_DIM], axis=1, keepdims=True)
    s1 = jnp.sum(xx[:, HEAD_DIM:], axis=1, keepdims=True)
    lane = lax.broadcasted_iota(jnp.int32, x.shape, 1)
    ms = jnp.where(lane < HEAD_DIM, s0, s1) * (1.0 / HEAD_DIM)
    return x * lax.rsqrt(ms + RMS_EPS) * gain


def _rotate_rows(x, cos, sin):
    n = cos.shape[0]
    x1, x2 = x[:n], x[n:]
    return x1 * cos - x2 * sin, x1 * sin + x2 * cos


def _proj_kernel(x_ref, wt_ref, ws_ref, tab_ref, tabt_ref, gk_ref, gq_ref,
                 qa_ref, ka_ref, va_ref, za_ref, qb_ref, kb_ref, vb_ref, zb_ref, g_ref):
    xb = x_ref[0].astype(BF16)
    tm = xb.shape[0]
    qscale = SCALE * LOG2E

    t = lax.dot_general(wt_ref[...], xb, (((1,), (1,)), ((), ())), preferred_element_type=F32)

    row = lax.broadcasted_iota(jnp.int32, (VT_ROWS - HEAD_DIM, tm), 0)
    ones_row = jnp.where(row == 0, 1.0, 0.0).astype(BF16)

    def put_v(ref, vt):
        for h in range(N_KV):
            ref[0, h, :HEAD_DIM, :] = vt[h * HEAD_DIM:(h + 1) * HEAD_DIM].astype(BF16)
            ref[0, h, HEAD_DIM:, :] = ones_row

    half, quarter = HEAD_DIM // 2, HEAD_DIM // 4
    cos_a, sin_a = tabt_ref[0:half, :], tabt_ref[half:2 * half, :]
    for h in range(N_HEADS):
        lo, hi = _rotate_rows(t[h * HEAD_DIM:(h + 1) * HEAD_DIM], cos_a, sin_a)
        qa_ref[0, h, :half, :] = (lo * qscale).astype(BF16)
        qa_ref[0, h, half:, :] = (hi * qscale).astype(BF16)
    put_v(va_ref, t[T_VA:T_VA + KV_WIDTH])

    base = 2 * half
    cos_r, sin_r = tabt_ref[base:base + quarter, :], tabt_ref[base + quarter:base + 2 * quarter, :]
    base += 2 * quarter
    cos_c, sin_c = tabt_ref[base:base + quarter, :], tabt_ref[base + quarter:base + 2 * quarter, :]
    gq = jnp.concatenate([gq_ref[...]] * (tm // LANES), axis=1)
    for h in range(N_HEADS):
        xh = t[T_QB + h * HEAD_DIM:T_QB + (h + 1) * HEAD_DIM]
        ms = jnp.sum(xh * xh, axis=0, keepdims=True) * (1.0 / HEAD_DIM)
        y = xh * lax.rsqrt(ms + RMS_EPS) * gq
        parts = _rotate_rows(y[:half], cos_r, sin_r) + _rotate_rows(y[half:], cos_c, sin_c)
        for i, part in enumerate(parts):
            qb_ref[0, h, i * quarter:(i + 1) * quarter, :] = (part * qscale).astype(BF16)
    put_v(vb_ref, t[T_VB:T_VB + KV_WIDTH])

    def mm(off, n):
        return jnp.dot(xb, ws_ref[:, off:off + n], preferred_element_type=F32)

    def tab(i):
        return tab_ref[:, i * LANES:(i + 1) * LANES]

    def put_k(ref, x):
        ref[0, 0] = x[:, :HEAD_DIM].astype(BF16)
        ref[0, 1] = x[:, HEAD_DIM:].astype(BF16)

    k = mm(S_KA, 2 * KV_WIDTH)
    put_k(ka_ref, _rope(k[:, :KV_WIDTH], tab(0), tab(1), tab(2), half))
    put_k(kb_ref, _rope(_head_rms(k[:, KV_WIDTH:], gk_ref[...]), tab(3), tab(4), tab(5), quarter))
    za_ref[0] = mm(S_ZA, WIDTH)
    zb_ref[0] = mm(S_ZB, WIDTH)
    for c in range(2 * D_MODEL // 512):
        g_ref[0, :, c * 512:(c + 1) * 512] = mm(S_G + c * 512, 512)


def _split_w_in(w):
    cols = lambda off, n: w[:, off:off + n]
    w_t = jnp.concatenate([cols(OFF_QA, WIDTH), cols(OFF_VA, KV_WIDTH),
                           cols(OFF_QB, WIDTH), cols(OFF_VB, KV_WIDTH)], axis=1).T
    w_s = jnp.concatenate([cols(OFF_KA, KV_WIDTH), cols(OFF_KB, KV_WIDTH), cols(OFF_ZA, WIDTH),
                           cols(OFF_ZB, WIDTH), cols(OFF_G, 2 * D_MODEL)], axis=1)
    return w_t.astype(BF16), w_s.astype(BF16)


def _project(x, w_t, w_s, tables, gk, gq, tm):
    tab, tab_t = tables
    bn, s, _ = x.shape
    grid = (s // tm, bn)
    tok = lambda n: pl.BlockSpec((1, tm, n), lambda i, b: (b, i, 0))
    kv = pl.BlockSpec((1, N_KV, tm, HEAD_DIM), lambda i, b: (b, 0, i, 0))
    kvt = pl.BlockSpec((1, N_KV, VT_ROWS, tm), lambda i, b: (b, 0, 0, i))
    qt = pl.BlockSpec((1, N_HEADS, HEAD_DIM, tm), lambda i, b: (b, 0, 0, i))
    const = lambda shape: pl.BlockSpec(shape, lambda i, b: (0,) * len(shape),
                                       pipeline_mode=pl.Buffered(1))
    q_shape = jax.ShapeDtypeStruct((bn, N_HEADS, HEAD_DIM, s), BF16)
    k_shape = jax.ShapeDtypeStruct((bn, N_KV, s, HEAD_DIM), BF16)
    v_shape = jax.ShapeDtypeStruct((bn, N_KV, VT_ROWS, s), BF16)
    z_shape = jax.ShapeDtypeStruct((bn, s, WIDTH), F32)
    out_shape = (q_shape, k_shape, v_shape, z_shape,
                 q_shape, k_shape, v_shape, z_shape,
                 jax.ShapeDtypeStruct((bn, s, 2 * D_MODEL), F32))
    return pl.pallas_call(
        _proj_kernel,
        grid=grid,
        in_specs=[tok(D_MODEL), const((T_ROWS, D_MODEL)), const((D_MODEL, S_COLS)),
                  pl.BlockSpec((tm, 6 * LANES), lambda i, b: (i, 0)),
                  pl.BlockSpec((LANES, tm), lambda i, b: (0, i)),
                  const((1, LANES)), const((HEAD_DIM, LANES))],
        out_specs=(qt, kv, kvt, tok(WIDTH), qt, kv, kvt, tok(WIDTH), tok(2 * D_MODEL)),
        out_shape=out_shape,
        compiler_params=pltpu.CompilerParams(
            dimension_semantics=("arbitrary", "arbitrary"), vmem_limit_bytes=VMEM_LIMIT),
        name="proj",
    )(x, w_t, w_s, tab, tab_t, gk, gq)


def _stack_heads(q):
    return jnp.concatenate([q[:, h * HEAD_DIM:(h + 1) * HEAD_DIM] for h in range(GROUP)], axis=0)


def _unstack_heads(o, tq):
    return jnp.concatenate([o[h * tq:(h + 1) * tq] for h in range(GROUP)], axis=1)


def _qk(q, k):
    return lax.dot_general(q, k, (((1,), (1,)), ((), ())), preferred_element_type=F32)


def _silu(z):
    return z * jax.nn.sigmoid(z)


def _attn_a_kernel(q_ref, kp_ref, kc_ref, kn_ref, vp_ref, vc_ref, vn_ref, km_ref, vm_ref,
                   sink_ref, z_ref, o_ref, *, nblk):
    assert nblk >= 2
    i = pl.program_id(2)
    last = pl.num_programs(2) - 1
    kwin =jnp.concatenate([kp_ref[0, 0], kc_ref[0, 0], kn_ref[0, 0]], axis=0)
    vwin = jnp.concatenate([vp_ref[0, 0], vc_ref[0, 0], vn_ref[0, 0]], axis=1)
    r = lax.broadcasted_iota(jnp.int32, (3 * BLOCK, BLOCK), 0)
    c = lax.broadcasted_iota(jnp.int32, (3 * BLOCK, BLOCK), 1)
    band = jnp.abs(r - BLOCK - c) <= WINDOW
    sink = sink_ref[0]
    for j in range(nblk):
        q4t = jnp.concatenate([q_ref[0, h, :, j * BLOCK:(j + 1) * BLOCK] for h in range(GROUP)],
                              axis=1)
        valid = band
        if j == 0:
            valid = valid & ((r >= BLOCK) | (i > 0))
        if j == nblk - 1:
            valid = valid & ((r < 2 * BLOCK) | (i < last))
        s = jnp.dot(kwin[j * BLOCK:(j + 3) * BLOCK], q4t, preferred_element_type=F32)
        s = jnp.concatenate(
            [jnp.where(valid, s[:, h * BLOCK:(h + 1) * BLOCK], NEG_INF) for h in range(GROUP)], axis=1)
        sm = jnp.dot(km_ref[0, 0], q4t, preferred_element_type=F32)
        m = jnp.maximum(jnp.maximum(jnp.max(s, axis=0, keepdims=True),
                                    jnp.max(sm, axis=0, keepdims=True)), sink)
        p = jnp.exp2(s - m).astype(BF16)
        pm = jnp.exp2(sm - m).astype(BF16)
        o_t = (jnp.dot(vwin[:, j * BLOCK:(j + 3) * BLOCK], p, preferred_element_type=F32)
               + jnp.dot(vm_ref[0, 0], pm, preferred_element_type=F32))
        l = o_t[HEAD_DIM:HEAD_DIM + 1] + jnp.exp2(sink - m)
        o_t = o_t[:HEAD_DIM] / l
        o = jnp.concatenate([o_t[:, h * BLOCK:(h + 1) * BLOCK] for h in range(GROUP)], axis=0).T
        rows = slice(j * BLOCK, (j + 1) * BLOCK)
        o_ref[0, rows, :] = (o * _silu(z_ref[0, rows, :])).astype(BF16)


def _attention_a(qa, ka, va, km, vm, sink_rows, za, nblk):
    bn, s = va.shape[0], va.shape[3]
    nb = s // BLOCK
    grid = (bn, N_KV, nb // nblk)
    tq = nblk * BLOCK
    qtspec = pl.BlockSpec((1, GROUP, HEAD_DIM, tq), lambda b, g, i: (b, g, 0, i))
    zspec = pl.BlockSpec((1, tq, GROUP * HEAD_DIM), lambda b, g, i: (b, i, g))
    prev = lambda i: jnp.maximum(i * nblk - 1, 0)
    nxt = lambda i: jnp.minimum((i + 1) * nblk, nb - 1)
    kprev = pl.BlockSpec((1, 1, BLOCK, HEAD_DIM), lambda b, g, i: (b, g, prev(i), 0))
    kcur = pl.BlockSpec((1, 1, tq, HEAD_DIM), lambda b, g, i: (b, g, i, 0))
    knext = pl.BlockSpec((1, 1, BLOCK, HEAD_DIM), lambda b, g, i: (b, g, nxt(i), 0))
    vprev = pl.BlockSpec((1, 1, VT_ROWS, BLOCK), lambda b, g, i: (b, g, 0, prev(i)))
    vcur = pl.BlockSpec((1, 1, VT_ROWS, tq), lambda b, g, i: (b, g, 0, i))
    vnext = pl.BlockSpec((1, 1, VT_ROWS, BLOCK), lambda b, g, i: (b, g, 0, nxt(i)))
    kmeta = pl.BlockSpec((1, 1, N_META, HEAD_DIM), lambda b, g, i: (0, g, 0, 0))
    vmeta = pl.BlockSpec((1, 1, VT_ROWS, N_META), lambda b, g, i: (0, g, 0, 0))
    sink = pl.BlockSpec((1, 1, GROUP * BLOCK), lambda b, g, i: (g, 0, 0))
    return pl.pallas_call(
        functools.partial(_attn_a_kernel, nblk=nblk),
        grid=grid,
        in_specs=[qtspec, kprev, kcur, knext, vprev, vcur, vnext, kmeta, vmeta, sink, zspec],
        out_specs=zspec,
        out_shape=jax.ShapeDtypeStruct((bn, s, WIDTH), BF16),
        compiler_params=pltpu.CompilerParams(
            dimension_semantics=("arbitrary", "arbitrary", "arbitrary"),
            vmem_limit_bytes=VMEM_LIMIT),
        name="attn_a",
    )(qa, ka, ka, ka, va, va, va, km, vm, sink_rows, za)


def _attn_b_kernel(q_ref, qn_ref, k_ref, v_ref, km_ref, vm_ref, z_ref, o_ref,
                   s0_ref, s1_ref, part_ref, sm_ref, *, tq, tk, sub):
    mq = GROUP * tq
    n_chunks = k_ref.shape[2] // tk
    assert n_chunks % 2 == 0
    s_refs = (s0_ref, s1_ref)

    def heads_on_lanes(ref):
        return jnp.concatenate([ref[0, h] for h in range(GROUP)], axis=1)

    def key_block(c, u):
        start = c * tk + u * sub
        return pl.ds(start if isinstance(start, int) else pl.multiple_of(start, sub), sub)

    q4t = heads_on_lanes(q_ref)

    def step(c, slot, m, part, acc, c_next, q_next, meta_now=False, meta_next=False):
        m_new = jnp.maximum(m, jnp.max(part, axis=0, keepdims=True))
        part_next, pv = None, None
        if meta_now:
            p_t = jnp.exp2(sm_ref[...] - m_new).astype(BF16)
            pv = jnp.dot(vm_ref[0, 0], p_t, preferred_element_type=F32)
        for u in range(tk // sub):
            rows = slice(u * sub, (u + 1) * sub)
            blk = jnp.dot(k_ref[0, 0, key_block(c_next, u), :], q_next,
                          preferred_element_type=F32)
            s_refs[1 - slot][rows, :] = blk
            bm = jnp.max(blk.reshape(sub // MAX_ROWS, MAX_ROWS, mq), axis=0)
            part_next = bm if part_next is None else jnp.maximum(part_next, bm)
            p_t = jnp.exp2(s_refs[slot][rows, :] - m_new).astype(BF16)
            d = jnp.dot(v_ref[0, 0, :, key_block(c, u)], p_t, preferred_element_type=F32)
            pv = d if pv is None else pv + d
        if meta_next:
            sm_ref[...] = jnp.dot(km_ref[0, 0], q_next, preferred_element_type=F32)
        return m_new, part_next, jnp.exp2(m - m_new) * acc + pv

    @pl.when(pl.program_id(2) == 0)
    def _():
        s0 = jnp.dot(k_ref[0, 0, 0:tk, :], q4t, preferred_element_type=F32)
        s0_ref[...] = s0
        part_ref[...] = jnp.max(s0.reshape(tk // MAX_ROWS, MAX_ROWS, mq), axis=0)
        sm_ref[...] = jnp.dot(km_ref[0, 0], q4t, preferred_element_type=F32)

    m = jnp.max(sm_ref[...], axis=0, keepdims=True)
    acc = jnp.zeros((VT_ROWS, mq), F32)

    branches = [
        lambda c, cr: step(0, 0, *cr, 1, q4t, meta_now=True),
        lambda c, cr: step(n_chunks - 1, 1, *cr, 0, heads_on_lanes(qn_ref), meta_next=True),
    ]
    if n_chunks > 2:
        branches += [lambda c, cr: step(c, 0, *cr, c + 1, q4t),
                     lambda c, cr: step(c, 1, *cr, c + 1, q4t)]

    def body(c, carry):
        kind = jnp.where(c == 0, 0, jnp.where(c == n_chunks - 1, 1, 2 + c % 2))
        return lax.switch(kind, [functools.partial(f, c) for f in branches], carry)

    _, part, acc = lax.fori_loop(0, n_chunks, body, (m, part_ref[...], acc))
    part_ref[...] = part
    o_t = acc[:HEAD_DIM] / acc[HEAD_DIM:HEAD_DIM + 1]
    o = jnp.concatenate([o_t[:, h * tq:(h + 1) * tq] for h in range(GROUP)], axis=0).T
    o_ref[0] = (o * _silu(z_ref[0])).astype(BF16)


def _attention_b(qb, kb, vb, km, vm, zb, tq, tk, sub):
    bn, _, s = vb.shape[0], vb.shape[1], vb.shape[3]
    grid = (bn, N_KV, s // tq)
    last = s // tq - 1
    qtspec = pl.BlockSpec((1, GROUP, HEAD_DIM, tq), lambda b, g, i: (b, g, 0, i))
    qnspec = pl.BlockSpec((1, GROUP, HEAD_DIM, tq), lambda b, g, i: (b, g, 0, jnp.minimum(i + 1, last)))
    qspec = pl.BlockSpec((1, tq, GROUP * HEAD_DIM), lambda b, g, i: (b, i, g))
    kspec = pl.BlockSpec((1, 1, s, HEAD_DIM), lambda b, g, i: (b, g, 0, 0))
    vspec = pl.BlockSpec((1, 1, VT_ROWS, s), lambda b, g, i: (b, g, 0, 0))
    kmeta = pl.BlockSpec((1, 1, N_META, HEAD_DIM), lambda b, g, i: (0, g, 0, 0))
    vmeta = pl.BlockSpec((1, 1, VT_ROWS, N_META), lambda b, g, i: (0, g, 0, 0))
    return pl.pallas_call(
        functools.partial(_attn_b_kernel, tq=tq, tk=tk, sub=sub),
        grid=grid,
        in_specs=[qtspec, qnspec, kspec, vspec, kmeta, vmeta, qspec],
        out_specs=qspec,
        out_shape=jax.ShapeDtypeStruct((bn, s, WIDTH), BF16),
        scratch_shapes=[pltpu.VMEM((tk, GROUP * tq), F32), pltpu.VMEM((tk, GROUP * tq), F32),
                        pltpu.VMEM((MAX_ROWS, GROUP * tq), F32),
                        pltpu.VMEM((N_META, GROUP * tq), F32)],
        compiler_params=pltpu.CompilerParams(
            dimension_semantics=("arbitrary", "arbitrary", "arbitrary"),
            vmem_limit_bytes=VMEM_LIMIT),
        name="attn_b",
    )(qb, qb, kb, vb, km, vm, zb)


def _out_kernel(ya_ref, yb_ref, g_ref, x_ref, wa_ref, wb_ref, wo_ref, ln_ref, o_ref):
    ua = jnp.dot(ya_ref[0], wa_ref[...], preferred_element_type=F32)
    ub = jnp.dot(yb_ref[0], wb_ref[...], preferred_element_type=F32)
    merged = (jax.nn.sigmoid(g_ref[0, :, :D_MODEL]) * ua
              + jax.nn.sigmoid(g_ref[0, :, D_MODEL:]) * ub)
    out = jnp.dot(merged.astype(BF16), wo_ref[...], preferred_element_type=F32)
    y = ALPHA * x_ref[0] + out
    mu = jnp.mean(y, axis=1, keepdims=True)
    d = y - mu
    var = jnp.mean(d * d, axis=1, keepdims=True)
    o_ref[0] = d * lax.rsqrt(var + LN_EPS) * ln_ref[0:1, :] + ln_ref[1:2, :]


def _output(ya, yb, gates, x, wa_bf, wb_bf, wo_bf, ln, tm):
    bn, s, _ = x.shape
    grid = (bn, s // tm)
    tok = lambda n: pl.BlockSpec((1, tm, n), lambda b, i: (b, i, 0))
    const = lambda shape: pl.BlockSpec(shape, lambda b, i: (0,) * len(shape),
                                       pipeline_mode=pl.Buffered(1))
    return pl.pallas_call(
        _out_kernel,
        grid=grid,
        in_specs=[tok(WIDTH), tok(WIDTH), tok(2 * D_MODEL), tok(D_MODEL),
                  const((WIDTH, D_MODEL)), const((WIDTH, D_MODEL)), const((D_MODEL, D_MODEL)),
                  const((2, D_MODEL))],
        out_specs=tok(D_MODEL),
        out_shape=jax.ShapeDtypeStruct((bn, s, D_MODEL), F32),
        compiler_params=pltpu.CompilerParams(
            dimension_semantics=("arbitrary", "arbitrary"), vmem_limit_bytes=VMEM_LIMIT),
        name="out",
    )(ya, yb, gates, x, wa_bf, wb_bf, wo_bf, ln)


def _encode(x, meta_kv, proj_args, tables, sink_rows, wa_bf, wb_bf, wo_bf, ln):
    kma, vma, kmb, vmb = meta_kv
    w_t, w_s, gk, gq = proj_args
    qa, ka, va, za, qb, kb, vb, zb, gates = _project(x, w_t, w_s, tables, gk, gq, tm=512)
    ya = _attention_a(qa, ka, va, kma, vma, sink_rows, za, nblk=16)
    yb = _attention_b(qb, kb, vb, kmb, vmb, zb, tq=128, tk=4096, sub=256)
    return _output(ya, yb, gates, x, wa_bf, wb_bf, wo_bf, ln, tm=512)


def kernel(x_prompt, x_sample, meta_tokens, w_in, attn_a_sink, q_norm_b, k_norm_b,
           w_branch_a, w_branch_b, w_out, ln_gain, ln_bias):
    assert w_in.shape[0] == DEPTH
    s = x_prompt.shape[1]
    assert x_sample.shape[1] == s and s % GRID_W == 0
    w_t, w_s = _split_w_in(w_in[0])
    wa_bf = w_branch_a[0].astype(BF16)
    wb_bf = w_branch_b[0].astype(BF16)
    wo_bf = w_out[0].astype(BF16)
    gk = jnp.tile(k_norm_b[0].astype(F32), 2)[None, :]
    gq = jnp.broadcast_to(q_norm_b[0].astype(F32)[:, None], (HEAD_DIM, LANES))
    proj_args = (w_t, w_s, gk, gq)
    ln = jnp.stack([ln_gain[0], ln_bias[0]]).astype(F32)
    sink_rows = jnp.repeat(attn_a_sink[0].astype(F32).reshape(N_KV, GROUP) * LOG2E, BLOCK,
                           axis=1)[:, None, :]

    meta_pos = jnp.arange(N_META) - N_META
    rows = s // GRID_W
    tables = _rope_tables(jnp.arange(s) + N_META,
                          jnp.repeat(jnp.arange(rows), GRID_W),
                          jnp.tile(jnp.arange(GRID_W), rows))
    pad = BLOCK - N_META
    meta_pos_pad = jnp.pad(meta_pos, (0, pad))
    meta_tables = _rope_tables(jnp.arange(BLOCK), meta_pos_pad, meta_pos_pad)
    meta_x = jnp.pad(meta_tokens.astype(F32), ((0, pad), (0, 0)))[None]
    mp = _project(meta_x, w_t, w_s, meta_tables, gk, gq, tm=BLOCK)
    meta_kv = (mp[1][:, :, :N_META], mp[2][:, :, :, :N_META], mp[5][:, :, :N_META],
               mp[6][:, :, :, :N_META])

    enc = functools.partial(_encode, meta_kv=meta_kv, proj_args=proj_args, tables=tables,
                            sink_rows=sink_rows, wa_bf=wa_bf, wb_bf=wb_bf, wo_bf=wo_bf, ln=ln)
    return (enc(x_prompt), enc(x_sample))
```

```python
import functools

import numpy as np
import jax
import jax.numpy as jnp
from jax import lax
from jax.experimental import pallas as pl
from jax.experimental.pallas import tpu as pltpu

D_MODEL = 1024
HEAD_DIM = 64
N_HEADS = 8
N_KV = 2
GROUP = N_HEADS // N_KV
WIDTH = N_HEADS * HEAD_DIM
KV_WIDTH = N_KV * HEAD_DIM
WINDOW = 128
BLOCK = 128
N_META = 16
GRID_W = 64
ROPE_THETA = 10000.0
LN_EPS = 1e-5
RMS_EPS = 1e-6
NEG_INF = -1e30
DEPTH = 1
ALPHA = (2.0 * DEPTH) ** 0.25
SCALE = HEAD_DIM ** -0.5
LOG2E = 1.4426950408889634
MAX_ROWS = 8

LANES = 128
VT_ROWS = HEAD_DIM + 16
VMEM_LIMIT = 56 * 1024 * 1024

OFF_QA = 0
OFF_KA = OFF_QA + WIDTH
OFF_VA = OFF_KA + KV_WIDTH
OFF_ZA = OFF_VA + KV_WIDTH
OFF_QB = OFF_ZA + WIDTH
OFF_KB = OFF_QB + WIDTH
OFF_VB = OFF_KB + KV_WIDTH
OFF_ZB = OFF_VB + KV_WIDTH
OFF_G = OFF_ZB + WIDTH
D_IN = OFF_G + 2 * D_MODEL

T_QA = 0
T_VA = T_QA + WIDTH
T_QB = T_VA + KV_WIDTH
T_VB = T_QB + WIDTH
T_ROWS = T_VB + KV_WIDTH
S_KA = 0
S_KB = S_KA + KV_WIDTH
S_ZA = S_KB + KV_WIDTH
S_ZB = S_ZA + WIDTH
S_G = S_ZB + WIDTH
S_COLS = S_G + 2 * D_MODEL

BF16 = jnp.bfloat16
F32 = jnp.float32


def _rope_angles(pos, dim):
    inv = ROPE_THETA ** (-jnp.arange(0, dim, 2, dtype=F32) / dim)
    return pos.astype(F32)[:, None] * inv[None, :]


def _rope_tables(pos_a, row, col):
    lane = np.arange(LANES)
    ang_a = _rope_angles(pos_a, HEAD_DIM)
    cos_a = jnp.tile(jnp.cos(ang_a), (1, 4))
    sin_a = jnp.tile(jnp.sin(ang_a), (1, 4))
    lo_a = jnp.asarray((lane % 64) < 32)[None, :]
    h = HEAD_DIM // 2
    ang_r = _rope_angles(row, h)
    ang_c = _rope_angles(col, h)
    cos_b = jnp.tile(jnp.concatenate([jnp.cos(ang_r)] * 2 + [jnp.cos(ang_c)] * 2, axis=1), (1, 2))
    sin_b = jnp.tile(jnp.concatenate([jnp.sin(ang_r)] * 2 + [jnp.sin(ang_c)] * 2, axis=1), (1, 2))
    lo_b = jnp.asarray((lane % 32) < 16)[None, :]
    zero = jnp.zeros_like(sin_a)
    token_major = jnp.concatenate([
        cos_a, jnp.where(lo_a, -sin_a, zero), jnp.where(lo_a, zero, sin_a),
        cos_b, jnp.where(lo_b, -sin_b, zero), jnp.where(lo_b, zero, sin_b)], axis=1)
    feature_major = jnp.concatenate([
        jnp.cos(ang_a), jnp.sin(ang_a), jnp.cos(ang_r), jnp.sin(ang_r),
        jnp.cos(ang_c), jnp.sin(ang_c)], axis=1).T
    return token_major, feature_major


def _rope(x, cos, sin_lo, sin_hi, half):
    return (x * cos + pltpu.roll(x, LANES - half, 1) * sin_lo
            + pltpu.roll(x, half, 1) * sin_hi)


def _head_rms(x, gain):
    xx = x * x
    s0 = jnp.sum(xx[:, :HEAD_DIM], axis=1, keepdims=True)
    s1 = jnp.sum(xx[:, HEAD_DIM:], axis=1, keepdims=True)
    lane = lax.broadcasted_iota(jnp.int32, x.shape, 1)
    ms = jnp.where(lane < HEAD_DIM, s0, s1) * (1.0 / HEAD_DIM)
    return x * lax.rsqrt(ms + RMS_EPS) * gain


def _rotate_rows(x, cos, sin):
    n = cos.shape[0]
    x1, x2 = x[:n], x[n:]
    return x1 * cos - x2 * sin, x1 * sin + x2 * cos


def _proj_kernel(x_ref, wt_ref, ws_ref, tab_ref, tabt_ref, gk_ref, gq_ref,
                 qa_ref, ka_ref, va_ref, za_ref, qb_ref, kb_ref, vb_ref, zb_ref, g_ref):
    xb = x_ref[0].astype(BF16)
    tm = xb.shape[0]
    qscale = SCALE * LOG2E

    t = lax.dot_general(wt_ref[...], xb, (((1,), (1,)), ((), ())), preferred_element_type=F32)

    row = lax.broadcasted_iota(jnp.int32, (VT_ROWS - HEAD_DIM, tm), 0)
    ones_row = jnp.where(row == 0, 1.0, 0.0).astype(BF16)

    def put_v(ref, vt):
        for h in range(N_KV):
            ref[0, h, :HEAD_DIM, :] = vt[h * HEAD_DIM:(h + 1) * HEAD_DIM].astype(BF16)
            ref[0, h, HEAD_DIM:, :] = ones_row

    half, quarter = HEAD_DIM // 2, HEAD_DIM // 4
    cos_a, sin_a = tabt_ref[0:half, :], tabt_ref[half:2 * half, :]
    for h in range(N_HEADS):
        lo, hi = _rotate_rows(t[h * HEAD_DIM:(h + 1) * HEAD_DIM], cos_a, sin_a)
        qa_ref[0, h, :half, :] = (lo * qscale).astype(BF16)
        qa_ref[0, h, half:, :] = (hi * qscale).astype(BF16)
    put_v(va_ref, t[T_VA:T_VA + KV_WIDTH])

    base = 2 * half
    cos_r, sin_r = tabt_ref[base:base + quarter, :], tabt_ref[base + quarter:base + 2 * quarter, :]
    base += 2 * quarter
    cos_c, sin_c = tabt_ref[base:base + quarter, :], tabt_ref[base + quarter:base + 2 * quarter, :]
    gq = jnp.concatenate([gq_ref[...]] * (tm // LANES), axis=1)
    for h in range(N_HEADS):
        xh = t[T_QB + h * HEAD_DIM:T_QB + (h + 1) * HEAD_DIM]
        ms = jnp.sum(xh * xh, axis=0, keepdims=True) * (1.0 / HEAD_DIM)
        y = xh * lax.rsqrt(ms + RMS_EPS) * gq
        parts = _rotate_rows(y[:half], cos_r, sin_r) + _rotate_rows(y[half:], cos_c, sin_c)
        for i, part in enumerate(parts):
            qb_ref[0, h, i * quarter:(i + 1) * quarter, :] = (part * qscale).astype(BF16)
    put_v(vb_ref, t[T_VB:T_VB + KV_WIDTH])

    def mm(off, n):
        return jnp.dot(xb, ws_ref[:, off:off + n], preferred_element_type=F32)

    def tab(i):
        return tab_ref[:, i * LANES:(i + 1) * LANES]

    def put_k(ref, x):
        ref[0, 0] = x[:, :HEAD_DIM].astype(BF16)
        ref[0, 1] = x[:, HEAD_DIM:].astype(BF16)

    k = mm(S_KA, 2 * KV_WIDTH)
    put_k(ka_ref, _rope(k[:, :KV_WIDTH], tab(0), tab(1), tab(2), half))
    put_k(kb_ref, _rope(_head_rms(k[:, KV_WIDTH:], gk_ref[...]), tab(3), tab(4), tab(5), quarter))
    za_ref[0] = mm(S_ZA, WIDTH)
    zb_ref[0] = mm(S_ZB, WIDTH)
    for c in range(2 * D_MODEL // 512):
        g_ref[0, :, c * 512:(c + 1) * 512] = mm(S_G + c * 512, 512)


def _split_w_in(w):
    cols = lambda off, n: w[:, off:off + n]
    w_t = jnp.concatenate([cols(OFF_QA, WIDTH), cols(OFF_VA, KV_WIDTH),
                           cols(OFF_QB, WIDTH), cols(OFF_VB, KV_WIDTH)], axis=1).T
    w_s = jnp.concatenate([cols(OFF_KA, KV_WIDTH), cols(OFF_KB, KV_WIDTH), cols(OFF_ZA, WIDTH),
                           cols(OFF_ZB, WIDTH), cols(OFF_G, 2 * D_MODEL)], axis=1)
    return w_t.astype(BF16), w_s.astype(BF16)


def _project(x, w_t, w_s, tables, gk, gq, tm):
    tab, tab_t = tables
    bn, s, _ = x.shape
    grid = (s // tm, bn)
    tok = lambda n: pl.BlockSpec((1, tm, n), lambda i, b: (b, i, 0))
    kv = pl.BlockSpec((1, N_KV, tm, HEAD_DIM), lambda i, b: (b, 0, i, 0))
    kvt = pl.BlockSpec((1, N_KV, VT_ROWS, tm), lambda i, b: (b, 0, 0, i))
    qt = pl.BlockSpec((1, N_HEADS, HEAD_DIM, tm), lambda i, b: (b, 0, 0, i))
    const = lambda shape: pl.BlockSpec(shape, lambda i, b: (0,) * len(shape),
                                       pipeline_mode=pl.Buffered(1))
    q_shape = jax.ShapeDtypeStruct((bn, N_HEADS, HEAD_DIM, s), BF16)
    k_shape = jax.ShapeDtypeStruct((bn, N_KV, s, HEAD_DIM), BF16)
    v_shape = jax.ShapeDtypeStruct((bn, N_KV, VT_ROWS, s), BF16)
    z_shape = jax.ShapeDtypeStruct((bn, s, WIDTH), F32)
    out_shape = (q_shape, k_shape, v_shape, z_shape,
                 q_shape, k_shape, v_shape, z_shape,
                 jax.ShapeDtypeStruct((bn, s, 2 * D_MODEL), F32))
    return pl.pallas_call(
        _proj_kernel,
        grid=grid,
        in_specs=[tok(D_MODEL), const((T_ROWS, D_MODEL)), const((D_MODEL, S_COLS)),
                  pl.BlockSpec((tm, 6 * LANES), lambda i, b: (i, 0)),
                  pl.BlockSpec((LANES, tm), lambda i, b: (0, i)),
                  const((1, LANES)), const((HEAD_DIM, LANES))],
        out_specs=(qt, kv, kvt, tok(WIDTH), qt, kv, kvt, tok(WIDTH), tok(2 * D_MODEL)),
        out_shape=out_shape,
        compiler_params=pltpu.CompilerParams(
            dimension_semantics=("arbitrary", "arbitrary"), vmem_limit_bytes=VMEM_LIMIT),
        name="proj",
    )(x, w_t, w_s, tab, tab_t, gk, gq)


def _stack_heads(q):
    return jnp.concatenate([q[:, h * HEAD_DIM:(h + 1) * HEAD_DIM] for h in range(GROUP)], axis=0)


def _unstack_heads(o, tq):
    return jnp.concatenate([o[h * tq:(h + 1) * tq] for h in range(GROUP)], axis=1)


def _qk(q, k):
    return lax.dot_general(q, k, (((1,), (1,)), ((), ())), preferred_element_type=F32)


def _silu(z):
    return z * jax.nn.sigmoid(z)


def _attn_a_kernel(q_ref, kp_ref, kc_ref, kn_ref, vp_ref, vc_ref, vn_ref, km_ref, vm_ref,
                   sink_ref, z_ref, o_ref, *, nblk):
    assert nblk >= 2
    i = pl.program_id(2)
    last = pl.num_programs(2) - 1
    kwin =jnp.concatenate([kp_ref[0, 0], kc_ref[0, 0], kn_ref[0, 0]], axis=0)
    vwin = jnp.concatenate([vp_ref[0, 0], vc_ref[0, 0], vn_ref[0, 0]], axis=1)
    r = lax.broadcasted_iota(jnp.int32, (3 * BLOCK, BLOCK), 0)
    c = lax.broadcasted_iota(jnp.int32, (3 * BLOCK, BLOCK), 1)
    band = jnp.abs(r - BLOCK - c) <= WINDOW
    sink = sink_ref[0]
    for j in range(nblk):
        q4t = jnp.concatenate([q_ref[0, h, :, j * BLOCK:(j + 1) * BLOCK] for h in range(GROUP)],
                              axis=1)
        valid = band
        if j == 0:
            valid = valid & ((r >= BLOCK) | (i > 0))
        if j == nblk - 1:
            valid = valid & ((r < 2 * BLOCK) | (i < last))
        s = jnp.dot(kwin[j * BLOCK:(j + 3) * BLOCK], q4t, preferred_element_type=F32)
        s = jnp.concatenate(
            [jnp.where(valid, s[:, h * BLOCK:(h + 1) * BLOCK], NEG_INF) for h in range(GROUP)], axis=1)
        sm = jnp.dot(km_ref[0, 0], q4t, preferred_element_type=F32)
        m = jnp.maximum(jnp.maximum(jnp.max(s, axis=0, keepdims=True),
                                    jnp.max(sm, axis=0, keepdims=True)), sink)
        p = jnp.exp2(s - m).astype(BF16)
        pm = jnp.exp2(sm - m).astype(BF16)
        o_t = (jnp.dot(vwin[:, j * BLOCK:(j + 3) * BLOCK], p, preferred_element_type=F32)
               + jnp.dot(vm_ref[0, 0], pm, preferred_element_type=F32))
        l = o_t[HEAD_DIM:HEAD_DIM + 1] + jnp.exp2(sink - m)
        o_t = o_t[:HEAD_DIM] / l
        o = jnp.concatenate([o_t[:, h * BLOCK:(h + 1) * BLOCK] for h in range(GROUP)], axis=0).T
        rows = slice(j * BLOCK, (j + 1) * BLOCK)
        o_ref[0, rows, :] = (o * _silu(z_ref[0, rows, :])).astype(BF16)


def _attention_a(qa, ka, va, km, vm, sink_rows, za, nblk):
    bn, s = va.shape[0], va.shape[3]
    nb = s // BLOCK
    grid = (bn, N_KV, nb // nblk)
    tq = nblk * BLOCK
    qtspec = pl.BlockSpec((1, GROUP, HEAD_DIM, tq), lambda b, g, i: (b, g, 0, i))
    zspec = pl.BlockSpec((1, tq, GROUP * HEAD_DIM), lambda b, g, i: (b, i, g))
    prev = lambda i: jnp.maximum(i * nblk - 1, 0)
    nxt = lambda i: jnp.minimum((i + 1) * nblk, nb - 1)
    kprev = pl.BlockSpec((1, 1, BLOCK, HEAD_DIM), lambda b, g, i: (b, g, prev(i), 0))
    kcur = pl.BlockSpec((1, 1, tq, HEAD_DIM), lambda b, g, i: (b, g, i, 0))
    knext = pl.BlockSpec((1, 1, BLOCK, HEAD_DIM), lambda b, g, i: (b, g, nxt(i), 0))
    vprev = pl.BlockSpec((1, 1, VT_ROWS, BLOCK), lambda b, g, i: (b, g, 0, prev(i)))
    vcur = pl.BlockSpec((1, 1, VT_ROWS, tq), lambda b, g, i: (b, g, 0, i))
    vnext = pl.BlockSpec((1, 1, VT_ROWS, BLOCK), lambda b, g, i: (b, g, 0, nxt(i)))
    kmeta = pl.BlockSpec((1, 1, N_META, HEAD_DIM), lambda b, g, i: (0, g, 0, 0))
    vmeta = pl.BlockSpec((1, 1, VT_ROWS, N_META), lambda b, g, i: (0, g, 0, 0))
    sink = pl.BlockSpec((1, 1, GROUP * BLOCK), lambda b, g, i: (g, 0, 0))
    return pl.pallas_call(
        functools.partial(_attn_a_kernel, nblk=nblk),
        grid=grid,
        in_specs=[qtspec, kprev, kcur, knext, vprev, vcur, vnext, kmeta, vmeta, sink, zspec],
        out_specs=zspec,
        out_shape=jax.ShapeDtypeStruct((bn, s, WIDTH), BF16),
        compiler_params=pltpu.CompilerParams(
            dimension_semantics=("arbitrary", "arbitrary", "arbitrary"),
            vmem_limit_bytes=VMEM_LIMIT),
        name="attn_a",
    )(qa, ka, ka, ka, va, va, va, km, vm, sink_rows, za)


def _attn_b_kernel(q_ref, qn_ref, k_ref, v_ref, km_ref, vm_ref, z_ref, o_ref,
                   s0_ref, s1_ref, part_ref, sm_ref, *, tq, tk, sub):
    mq = GROUP * tq
    n_chunks = k_ref.shape[2] // tk
    assert n_chunks == 1 or n_chunks % 2 == 0
    s_refs = (s0_ref, s1_ref)

    def heads_on_lanes(ref):
        return jnp.concatenate([ref[0, h] for h in range(GROUP)], axis=1)

    def key_block(c, u):
        start = c * tk + u * sub
        return pl.ds(start if isinstance(start, int) else pl.multiple_of(start, sub), sub)

    q4t = heads_on_lanes(q_ref)

    def step(c, slot, m, part, acc, c_next, q_next, meta_now=False, meta_next=False):
        m_new = jnp.maximum(m, jnp.max(part, axis=0, keepdims=True))
        part_next, pv = None, None
        if meta_now:
            p_t = jnp.exp2(sm_ref[...] - m_new).astype(BF16)
            pv = jnp.dot(vm_ref[0, 0], p_t, preferred_element_type=F32)
        for u in range(tk // sub):
            rows = slice(u * sub, (u + 1) * sub)
            blk = jnp.dot(k_ref[0, 0, key_block(c_next, u), :], q_next,
                          preferred_element_type=F32)
            s_refs[1 - slot][rows, :] = blk
            bm = jnp.max(blk.reshape(sub // MAX_ROWS, MAX_ROWS, mq), axis=0)
            part_next = bm if part_next is None else jnp.maximum(part_next, bm)
            p_t = jnp.exp2(s_refs[slot][rows, :] - m_new).astype(BF16)
            d = jnp.dot(v_ref[0, 0, :, key_block(c, u)], p_t, preferred_element_type=F32)
            pv = d if pv is None else pv + d
        if meta_next:
            sm_ref[...] = jnp.dot(km_ref[0, 0], q_next, preferred_element_type=F32)
        return m_new, part_next, jnp.exp2(m - m_new) * acc + pv

    @pl.when(pl.program_id(2) == 0)
    def _():
        part = None
        for u in range(tk // sub):
            blk = jnp.dot(k_ref[0, 0, key_block(0, u), :], q4t, preferred_element_type=F32)
            s0_ref[u * sub:(u + 1) * sub, :] = blk
            bm = jnp.max(blk.reshape(sub // MAX_ROWS, MAX_ROWS, mq), axis=0)
            part = bm if part is None else jnp.maximum(part, bm)
        part_ref[...] = part
        sm_ref[...] = jnp.dot(km_ref[0, 0], q4t, preferred_element_type=F32)

    m = jnp.max(sm_ref[...], axis=0, keepdims=True)
    acc = jnp.zeros((VT_ROWS, mq), F32)
    carry = (m, part_ref[...], acc)

    if n_chunks == 1:
        q_next = heads_on_lanes(qn_ref)
        _, part, acc = lax.cond(
            pl.program_id(2) % 2 == 0,
            lambda cr: step(0, 0, *cr, 0, q_next, meta_now=True, meta_next=True),
            lambda cr: step(0, 1, *cr, 0, q_next, meta_now=True, meta_next=True), carry)
    else:
        branches = [
            lambda c, cr: step(0, 0, *cr, 1, q4t, meta_now=True),
            lambda c, cr: step(n_chunks - 1, 1, *cr, 0, heads_on_lanes(qn_ref), meta_next=True),
        ]
        if n_chunks > 2:
            branches += [lambda c, cr: step(c, 0, *cr, c + 1, q4t),
                         lambda c, cr: step(c, 1, *cr, c + 1, q4t)]

        def body(c, cr):
            kind = jnp.where(c == 0, 0, jnp.where(c == n_chunks - 1, 1, 2 + c % 2))
            return lax.switch(kind, [functools.partial(f, c) for f in branches], cr)

        _, part, acc = lax.fori_loop(0, n_chunks, body, carry)
    part_ref[...] = part
    o_t = acc[:HEAD_DIM] / acc[HEAD_DIM:HEAD_DIM + 1]
    o = jnp.concatenate([o_t[:, h * tq:(h + 1) * tq] for h in range(GROUP)], axis=0).T
    o_ref[0] = (o * _silu(z_ref[0])).astype(BF16)


def _attention_b(qb, kb, vb, km, vm, zb, tq, tk, sub):
    bn, _, s = vb.shape[0], vb.shape[1], vb.shape[3]
    grid = (bn, N_KV, s // tq)
    last = s // tq - 1
    qtspec = pl.BlockSpec((1, GROUP, HEAD_DIM, tq), lambda b, g, i: (b, g, 0, i))
    qnspec = pl.BlockSpec((1, GROUP, HEAD_DIM, tq), lambda b, g, i: (b, g, 0, jnp.minimum(i + 1, last)))
    qspec = pl.BlockSpec((1, tq, GROUP * HEAD_DIM), lambda b, g, i: (b, i, g))
    kspec = pl.BlockSpec((1, 1, s, HEAD_DIM), lambda b, g, i: (b, g, 0, 0))
    vspec = pl.BlockSpec((1, 1, VT_ROWS, s), lambda b, g, i: (b, g, 0, 0))
    kmeta = pl.BlockSpec((1, 1, N_META, HEAD_DIM), lambda b, g, i: (0, g, 0, 0))
    vmeta = pl.BlockSpec((1, 1, VT_ROWS, N_META), lambda b, g, i: (0, g, 0, 0))
    return pl.pallas_call(
        functools.partial(_attn_b_kernel, tq=tq, tk=tk, sub=sub),
        grid=grid,
        in_specs=[qtspec, qnspec, kspec, vspec, kmeta, vmeta, qspec],
        out_specs=qspec,
        out_shape=jax.ShapeDtypeStruct((bn, s, WIDTH), BF16),
        scratch_shapes=[pltpu.VMEM((tk, GROUP * tq), F32), pltpu.VMEM((tk, GROUP * tq), F32),
                        pltpu.VMEM((MAX_ROWS, GROUP * tq), F32),
                        pltpu.VMEM((N_META, GROUP * tq), F32)],
        compiler_params=pltpu.CompilerParams(
            dimension_semantics=("arbitrary", "arbitrary", "arbitrary"),
            vmem_limit_bytes=VMEM_LIMIT),
        name="attn_b",
    )(qb, qb, kb, vb, km, vm, zb)


def _out_kernel(ya_ref, yb_ref, g_ref, x_ref, wa_ref, wb_ref, wo_ref, ln_ref, o_ref):
    ua = jnp.dot(ya_ref[0], wa_ref[...], preferred_element_type=F32)
    ub = jnp.dot(yb_ref[0], wb_ref[...], preferred_element_type=F32)
    merged = (jax.nn.sigmoid(g_ref[0, :, :D_MODEL]) * ua
              + jax.nn.sigmoid(g_ref[0, :, D_MODEL:]) * ub)
    out = jnp.dot(merged.astype(BF16), wo_ref[...], preferred_element_type=F32)
    y = ALPHA * x_ref[0] + out
    mu = jnp.mean(y, axis=1, keepdims=True)
    d = y - mu
    var = jnp.mean(d * d, axis=1, keepdims=True)
    o_ref[0] = d * lax.rsqrt(var + LN_EPS) * ln_ref[0:1, :] + ln_ref[1:2, :]


def _output(ya, yb, gates, x, wa_bf, wb_bf, wo_bf, ln, tm):
    bn, s, _ = x.shape
    grid = (bn, s // tm)
    tok = lambda n: pl.BlockSpec((1, tm, n), lambda b, i: (b, i, 0))
    const = lambda shape: pl.BlockSpec(shape, lambda b, i: (0,) * len(shape),
                                       pipeline_mode=pl.Buffered(1))
    return pl.pallas_call(
        _out_kernel,
        grid=grid,
        in_specs=[tok(WIDTH), tok(WIDTH), tok(2 * D_MODEL), tok(D_MODEL),
                  const((WIDTH, D_MODEL)), const((WIDTH, D_MODEL)), const((D_MODEL, D_MODEL)),
                  const((2, D_MODEL))],
        out_specs=tok(D_MODEL),
        out_shape=jax.ShapeDtypeStruct((bn, s, D_MODEL), F32),
        compiler_params=pltpu.CompilerParams(
            dimension_semantics=("arbitrary", "arbitrary"), vmem_limit_bytes=VMEM_LIMIT),
        name="out",
    )(ya, yb, gates, x, wa_bf, wb_bf, wo_bf, ln)


def _encode(x, meta_kv, proj_args, tables, sink_rows, wa_bf, wb_bf, wo_bf, ln):
    kma, vma, kmb, vmb = meta_kv
    w_t, w_s, gk, gq = proj_args
    qa, ka, va, za, qb, kb, vb, zb, gates = _project(x, w_t, w_s, tables, gk, gq, tm=512)
    ya = _attention_a(qa, ka, va, kma, vma, sink_rows, za, nblk=16)
    yb = _attention_b(qb, kb, vb, kmb, vmb, zb, tq=128, tk=x.shape[1], sub=256)
    return _output(ya, yb, gates, x, wa_bf, wb_bf, wo_bf, ln, tm=512)


def kernel(x_prompt, x_sample, meta_tokens, w_in, attn_a_sink, q_norm_b, k_norm_b,
           w_branch_a, w_branch_b, w_out, ln_gain, ln_bias):
    assert w_in.shape[0] == DEPTH
    s = x_prompt.shape[1]
    assert x_sample.shape[1] == s and s % GRID_W == 0
    w_t, w_s = _split_w_in(w_in[0])
    wa_bf = w_branch_a[0].astype(BF16)
    wb_bf = w_branch_b[0].astype(BF16)
    wo_bf = w_out[0].astype(BF16)
    gk = jnp.tile(k_norm_b[0].astype(F32), 2)[None, :]
    gq = jnp.broadcast_to(q_norm_b[0].astype(F32)[:, None], (HEAD_DIM, LANES))
    proj_args = (w_t, w_s, gk, gq)
    ln = jnp.stack([ln_gain[0], ln_bias[0]]).astype(F32)
    sink_rows = jnp.repeat(attn_a_sink[0].astype(F32).reshape(N_KV, GROUP) * LOG2E, BLOCK,
                           axis=1)[:, None, :]

    meta_pos = jnp.arange(N_META) - N_META
    rows = s // GRID_W
    tables = _rope_tables(jnp.arange(s) + N_META,
                          jnp.repeat(jnp.arange(rows), GRID_W),
                          jnp.tile(jnp.arange(GRID_W), rows))
    pad = BLOCK - N_META
    meta_pos_pad = jnp.pad(meta_pos, (0, pad))
    meta_tables = _rope_tables(jnp.arange(BLOCK), meta_pos_pad, meta_pos_pad)
    meta_x = jnp.pad(meta_tokens.astype(F32), ((0, pad), (0, 0)))[None]
    mp = _project(meta_x, w_t, w_s, meta_tables, gk, gq, tm=BLOCK)
    meta_kv = (mp[1][:, :, :N_META], mp[2][:, :, :, :N_META], mp[5][:, :, :N_META],
               mp[6][:, :, :, :N_META])

    enc = functools.partial(_encode, meta_kv=meta_kv, proj_args=proj_args, tables=tables,
                            sink_rows=sink_rows, wa_bf=wa_bf, wb_bf=wb_bf, wo_bf=wo_bf, ln=ln)
    return (enc(x_prompt), enc(x_sample))
```
